```python
import jax, jax.numpy as jnp
from jax import lax
import numpy as np

D_MODEL = 1024
BATCH = 8
SEQ = 4096
DEPTH = 2

GRID_W = 64
CTX_LEN = 256
HEAD_DIM = 64
N_Q_HEADS = 8
N_KV_HEADS = 2
Q_GROUP = N_Q_HEADS // N_KV_HEADS
ATTN_WIDTH = N_Q_HEADS * HEAD_DIM
KV_WIDTH = N_KV_HEADS * HEAD_DIM
CHUNK = 128
N_SG_GROUPS = 4
SG_WIDTH = D_MODEL - ATTN_WIDTH
SG_GROUP_DIM = SG_WIDTH // N_SG_GROUPS
IN_WIDTH = ATTN_WIDTH + 2 * KV_WIDTH + 2 * SG_WIDTH
Q_BLOCK = 128
ROPE_THETA = 10000.0
AXIS_ROPE_DIM = HEAD_DIM // 2
CONV_WIDTH = 31
D_FF = ((8 * D_MODEL // 3 + 255) // 256) * 256
N_EVEN = (DEPTH + 1) // 2
N_ODD = DEPTH // 2
EPS = 1e-6

kernel_name = "hybrid_attn_sgmlp_conformer_dit"


def rms_norm(x, g):
    xf = x.astype(jnp.float32)
    y = xf * lax.rsqrt(jnp.mean(xf * xf, axis=-1, keepdims=True) + EPS)
    return (y * g.astype(jnp.float32)).astype(x.dtype)


def layer_norm(x, g=None, b=None):
    xf = x.astype(jnp.float32)
    mu = jnp.mean(xf, axis=-1, keepdims=True)
    var = jnp.mean(jnp.square(xf - mu), axis=-1, keepdims=True)
    y = (xf - mu) * lax.rsqrt(var + EPS)
    if g is not None:
        y = y * g.astype(jnp.float32) + b.astype(jnp.float32)
    return y.astype(x.dtype)


def axial_rope_tables(n):
    rows = n // GRID_W
    row = jnp.broadcast_to(jnp.arange(rows)[:, None], (rows, GRID_W)).reshape(-1).astype(jnp.float32)
    col = jnp.broadcast_to(jnp.arange(GRID_W)[None, :], (rows, GRID_W)).reshape(-1).astype(jnp.float32)
    inv = ROPE_THETA ** (-jnp.arange(0, AXIS_ROPE_DIM, 2, dtype=jnp.float32) / AXIS_ROPE_DIM)
    ang_r = row[:, None] * inv[None, :]
    ang_c = col[:, None] * inv[None, :]
    return (jnp.cos(ang_r), jnp.sin(ang_r), jnp.cos(ang_c), jnp.sin(ang_c))


def rotate_half(x, cos, sin):
    x1, x2 = jnp.split(x, 2, axis=-1)
    cos = cos[None, :, None, :]
    sin = sin[None, :, None, :]
    return jnp.concatenate([x1 * cos - x2 * sin, x2 * cos + x1 * sin], axis=-1)


def apply_axial_rope(x, tables):
    cos_r, sin_r, cos_c, sin_c = tables
    xf = x.astype(jnp.float32)
    xr, xc = jnp.split(xf, 2, axis=-1)
    out = jnp.concatenate([rotate_half(xr, cos_r, sin_r), rotate_half(xc, cos_c, sin_c)], axis=-1)
    return out.astype(x.dtype)


def block_attention(q, k, v):
    b, n = q.shape[0], q.shape[1]
    nb = n // Q_BLOCK
    qb = (q * HEAD_DIM ** -0.5).reshape(b, nb, Q_BLOCK, N_KV_HEADS, Q_GROUP, HEAD_DIM)
    qb = qb.transpose(1, 0, 2, 3, 4, 5)

    def one_block(q_blk):
        s = jnp.einsum('bqkgd,bskd->bkgqs', q_blk, k).astype(jnp.float32)
        p = jax.nn.softmax(s, axis=-1).astype(v.dtype)
        return jnp.einsum('bkgqs,bskd->bqkgd', p, v)

    o = lax.map(one_block, qb)
    return o.transpose(1, 0, 2, 3, 4, 5).reshape(b, n, ATTN_WIDTH)


def spatial_gating(u, v, w_sp, b_sp):
    b, n = u.shape[0], u.shape[1]
    shape = (b, n // CHUNK, CHUNK, N_SG_GROUPS, SG_GROUP_DIM)
    vn = layer_norm(v.reshape(shape))
    mixed = jnp.einsum('gpq,bmqgc->bmpgc', w_sp, vn) + b_sp.T[:, :, None]
    return (u.reshape(shape) * mixed).reshape(b, n, SG_WIDTH)


def split_in_proj(p):
    o1 = ATTN_WIDTH
    o2 = o1 + KV_WIDTH
    o3 = o2 + KV_WIDTH
    o4 = o3 + SG_WIDTH
    return jnp.split(p, [o1, o2, o3, o4], axis=-1)


def even_mixer(xm, xc, w_in, q_gain, k_gain, w_sp, b_sp, w_out, rope, ctx_out):
    b, n, _ = xm.shape
    bc, m, _ = xc.shape
    q, k, v, su, sv = split_in_proj(xm @ w_in)
    q = apply_axial_rope(rms_norm(q.reshape(b, n, N_Q_HEADS, HEAD_DIM), q_gain), rope)
    k = apply_axial_rope(rms_norm(k.reshape(b, n, N_KV_HEADS, HEAD_DIM), k_gain), rope)
    v = v.reshape(b, n, N_KV_HEADS, HEAD_DIM)
    if ctx_out:
        qc, kc, vc, suc, svc = split_in_proj(xc @ w_in)
    else:
        kc, vc = jnp.split(xc @ w_in[:, ATTN_WIDTH:ATTN_WIDTH + 2 * KV_WIDTH], 2, axis=-1)
    kc = rms_norm(kc.reshape(bc, m, N_KV_HEADS, HEAD_DIM), k_gain)
    vc = vc.reshape(bc, m, N_KV_HEADS, HEAD_DIM)
    attn = block_attention(q, jnp.concatenate([kc, k], axis=1), jnp.concatenate([vc, v], axis=1))
    sg = spatial_gating(jax.nn.gelu(su), jax.nn.gelu(sv), w_sp, b_sp)
    y = jnp.concatenate([attn, sg], axis=-1) @ w_out
    yc = None
    if ctx_out:
        qc = rms_norm(qc.reshape(bc, m, N_Q_HEADS, HEAD_DIM), q_gain)
        attn_c = block_attention(qc, kc, vc)
        sg_c = spatial_gating(jax.nn.gelu(suc), jax.nn.gelu(svc), w_sp, b_sp)
        yc = jnp.concatenate([attn_c, sg_c], axis=-1) @ w_out
    return y, yc


def conformer_conv(x, w_pw1, b_pw1, w_dw, b_dw, ln_g, ln_b, w_pw2, b_pw2):
    a, gate = jnp.split(x @ w_pw1 + b_pw1, 2, axis=-1)
    h = a * jax.nn.sigmoid(gate)
    h = lax.conv_general_dilated(h, w_dw[:, None, :].astype(h.dtype), window_strides=(1,),
                                 padding=[(CONV_WIDTH // 2, CONV_WIDTH // 2)],
                                 dimension_numbers=('NWC', 'WIO', 'NWC'),
                                 feature_group_count=D_MODEL) + b_dw
    h = jax.nn.silu(layer_norm(h, ln_g, ln_b))
    return h @ w_pw2 + b_pw2


def swiglu_ffn(x, w_in, w_out):
    g, u = jnp.split(x @ w_in, 2, axis=-1)
    return (jax.nn.silu(g) * u) @ w_out


def adaln(cvec, w_mod, b_mod):
    return jnp.split(jax.nn.silu(cvec) @ w_mod + b_mod, 6, axis=-1)


def setup_inputs(seed: int = 0) -> dict:
    key = jax.random.key(seed)
    ks = iter(jax.random.split(key, 32))
    f32 = jnp.float32

    def nrm(shape, scale):
        return jax.random.normal(next(ks), shape, f32) * scale

    D = D_MODEL
    return {
        'x': nrm((BATCH, SEQ, D), 1.0),
        'c': nrm((BATCH, D), 1.0),
        'ctx': nrm((BATCH, CTX_LEN, D), 1.0),
        'c_ctx': nrm((D,), 1.0),
        'w_mod': nrm((DEPTH, D, 6 * D), 0.5 * D ** -0.5),
        'b_mod': nrm((DEPTH, 6 * D), 0.01),
        'g_mix': 1.0 + nrm((DEPTH, D), 0.01),
        'g_ffn': 1.0 + nrm((DEPTH, D), 0.01),
        'w_ffn_in': nrm((DEPTH, D, 2 * D_FF), D ** -0.5),
        'w_ffn_out': nrm((DEPTH, D_FF, D), D_FF ** -0.5),
        'w_in': nrm((N_EVEN, D, IN_WIDTH), D ** -0.5),
        'q_gain': 1.0 + nrm((N_EVEN, HEAD_DIM), 0.01),
        'k_gain': 1.0 + nrm((N_EVEN, HEAD_DIM), 0.01),
        'w_sp': nrm((N_EVEN, N_SG_GROUPS, CHUNK, CHUNK), CHUNK ** -0.5),
        'b_sp': 1.0 + nrm((N_EVEN, N_SG_GROUPS, CHUNK), 0.01),
        'w_out': nrm((N_EVEN, ATTN_WIDTH + SG_WIDTH, D), (ATTN_WIDTH + SG_WIDTH) ** -0.5),
        'w_pw1': nrm((N_ODD, D, 2 * D), D ** -0.5),
        'b_pw1': nrm((N_ODD, 2 * D), 0.01),
        'w_dw': nrm((N_ODD, CONV_WIDTH, D), CONV_WIDTH ** -0.5),
        'b_dw': nrm((N_ODD, D), 0.01),
        'ln_g': 1.0 + nrm((N_ODD, D), 0.01),
        'ln_b': nrm((N_ODD, D), 0.01),
        'w_pw2': nrm((N_ODD, D, D), D ** -0.5),
        'b_pw2': nrm((N_ODD, D), 0.01),
        'g_final': 1.0 + nrm((D,), 0.01),
    }


def reference(x, c, ctx, c_ctx, w_mod, b_mod, g_mix, g_ffn, w_ffn_in, w_ffn_out,
              w_in, q_gain, k_gain, w_sp, b_sp, w_out,
              w_pw1, b_pw1, w_dw, b_dw, ln_g, ln_b, w_pw2, b_pw2, g_final):
    n = x.shape[1]
    rope = axial_rope_tables(n)
    h, hc = x, ctx
    for l in range(DEPTH):
        even = (l % 2 == 0)
        i = l // 2
        ctx_after = any(j % 2 == 0 for j in range(l + 1, DEPTH))
        sh1, sc1, gt1, sh2, sc2, gt2 = [t[:, None, :] for t in adaln(c, w_mod[l], b_mod[l])]
        xm = rms_norm(h, g_mix[l]) * (1.0 + sc1) + sh1
        if even or ctx_after:
            csh1, csc1, cgt1, csh2, csc2, cgt2 = adaln(c_ctx, w_mod[l], b_mod[l])
            xc = rms_norm(hc, g_mix[l]) * (1.0 + csc1) + csh1
        if even:
            y, yc = even_mixer(xm, xc, w_in[i], q_gain[i], k_gain[i], w_sp[i], b_sp[i], w_out[i],
                               rope, ctx_after)
        else:
            conv_p = (w_pw1[i], b_pw1[i], w_dw[i], b_dw[i], ln_g[i], ln_b[i], w_pw2[i], b_pw2[i])
            y = conformer_conv(xm, *conv_p)
            yc = conformer_conv(xc, *conv_p) if ctx_after else None
        h = h + gt1 * y
        h = h + gt2 * swiglu_ffn(rms_norm(h, g_ffn[l]) * (1.0 + sc2) + sh2, w_ffn_in[l], w_ffn_out[l])
        if ctx_after:
            hc = hc + cgt1 * yc
            hc = hc + cgt2 * swiglu_ffn(rms_norm(hc, g_ffn[l]) * (1.0 + csc2) + csh2,
                                        w_ffn_in[l], w_ffn_out[l])
    return rms_norm(h, g_final)
```

```python
import functools

import jax
import jax.numpy as jnp
from jax import lax
from jax.experimental import pallas as pl
from jax.experimental.pallas import tpu as pltpu

F32 = jnp.float32
BF16 = jnp.bfloat16

HEAD_DIM = 64
N_Q_HEADS = 8
N_KV_HEADS = 2
GRID_W = 64
CHUNK = 128
N_SG_GROUPS = 4
ROPE_THETA = 10000.0
CONV_WIDTH = 31
EPS = 1e-6

LANES = 128
HALO = 16
FFN_CHUNK = 256
VMEM_LIMIT = 56 * 1024 * 1024

TM = 512
TQ = 256
TK = 512


def _const_spec(shape):
    nd = len(shape)
    return pl.BlockSpec(shape, lambda *_: (0,) * nd, pipeline_mode=pl.Buffered(1))


def _params(n_axes):
    return pltpu.CompilerParams(dimension_semantics=("arbitrary",) * n_axes,
                                vmem_limit_bytes=VMEM_LIMIT)


def _dot(a, b):
    return jnp.dot(a, b, preferred_element_type=F32)


def _sigmoid(x):
    return 1.0 / (1.0 + jnp.exp(-x))


def _gelu_tanh(x):
    c = 0.7978845608028654
    return x * (0.5 * (1.0 + jnp.tanh(c * (x + 0.044715 * (x * x * x)))))


def _rms_mod(x, g, sc, sh):
    ms = jnp.mean(x * x, axis=-1, keepdims=True)
    return (x * lax.rsqrt(ms + EPS) * g) * (1.0 + sc) + sh


def _head_rms(z, bd, gain):
    z2 = z * z
    hi = z2.astype(BF16)
    lo = (z2 - hi.astype(F32)).astype(BF16)
    ssq = _dot(hi, bd) + _dot(lo, bd)
    return z * lax.rsqrt(ssq * (1.0 / HEAD_DIM) + EPS) * gain


def _rope(z, cos, s_up, s_dn):
    return z * cos + pltpu.roll(z, LANES - 16, 1) * s_up + pltpu.roll(z, 16, 1) * s_dn


def _adaln_kernel(c_ref, w_ref, b_ref, o_ref):
    c = c_ref[...]
    s = (c * _sigmoid(c)).astype(BF16)
    o_ref[...] = _dot(s, w_ref[...].astype(BF16)) + b_ref[...]


def _adaln(cvec, w_mod, b_mod):
    depth, d, n = w_mod.shape
    rows = cvec.shape[0]
    tn = 1536
    return pl.pallas_call(
        _adaln_kernel,
        grid=(depth, n // tn),
        in_specs=[
            pl.BlockSpec((rows, d), lambda l, j: (0, 0)),
            pl.BlockSpec((None, d, tn), lambda l, j: (l, 0, j)),
            pl.BlockSpec((None, 1, tn), lambda l, j: (l, 0, j)),
        ],
        out_specs=pl.BlockSpec((None, rows, tn), lambda l, j: (l, 0, j)),
        out_shape=jax.ShapeDtypeStruct((depth, rows, n), F32),
        compiler_params=_params(2),
        name="adaln",
    )(cvec, w_mod, b_mod.reshape(depth, 1, n))


def _l0_pre_kernel(x_ref, mod_ref, g_ref, w_ref, qg_ref, kg_ref, cos_ref, sup_ref, sdn_ref, bd_ref,
                   wsp_ref, bsp_ref, q_ref, k_ref, v_ref, sg_ref):
    tm = x_ref.shape[0]
    aw = N_Q_HEADS * HEAD_DIM
    kvw = N_KV_HEADS * HEAD_DIM
    sgw = N_SG_GROUPS * LANES
    xm = _rms_mod(x_ref[...], g_ref[...], mod_ref[1:2, :], mod_ref[0:1, :]).astype(BF16)
    p = _dot(xm, w_ref[...])
    bd = bd_ref[...]
    cos, sup, sdn = cos_ref[...], sup_ref[...], sdn_ref[...]
    for j in range(aw // LANES):
        qn = _head_rms(p[:, j * LANES:(j + 1) * LANES], bd, qg_ref[...])
        q_ref[j] = (_rope(qn, cos, sup, sdn) * HEAD_DIM ** -0.5).astype(BF16)
    kn = _head_rms(p[:, aw:aw + kvw], bd, kg_ref[...])
    k_ref[...] = _rope(kn, cos, sup, sdn).astype(BF16)
    v_ref[...] = p[:, aw + kvw:aw + 2 * kvw].astype(BF16)
    o_u = aw + 2 * kvw
    o_v = o_u + sgw
    nchunk = tm // CHUNK
    for g in range(N_SG_GROUPS):
        gv = _gelu_tanh(p[:, o_v + g * LANES:o_v + (g + 1) * LANES])
        mu = jnp.mean(gv, axis=-1, keepdims=True)
        dv = gv - mu
        var = jnp.mean(dv * dv, axis=-1, keepdims=True)
        vn = (dv * lax.rsqrt(var + EPS)).astype(BF16)
        rhs = jnp.concatenate([vn[c * CHUNK:(c + 1) * CHUNK, :] for c in range(nchunk)], axis=1)
        mixed = _dot(wsp_ref[g], rhs)
        gu = _gelu_tanh(p[:, o_u + g * LANES:o_u + (g + 1) * LANES])
        for c in range(nchunk):
            blk = mixed[:, c * CHUNK:(c + 1) * CHUNK] + bsp_ref[g]
            sg_ref[c * CHUNK:(c + 1) * CHUNK, g * LANES:(g + 1) * LANES] = (
                gu[c * CHUNK:(c + 1) * CHUNK, :] * blk).astype(BF16)


def _l0_pre(x, mod, g_mix, w_in, qg2, kg2, cos, sup, sdn, bd, w_sp, bsp):
    b, s, d = x.shape
    inw = w_in.shape[1]
    npair = N_Q_HEADS * HEAD_DIM // LANES
    kvw = N_KV_HEADS * HEAD_DIM
    sgw = N_SG_GROUPS * LANES
    row = lambda bi, i: (bi, i, 0)
    tab = pl.BlockSpec((TM, LANES), lambda bi, i: (i, 0))
    return pl.pallas_call(
        _l0_pre_kernel,
        grid=(b, s // TM),
        in_specs=[
            pl.BlockSpec((None, TM, d), row),
            pl.BlockSpec((None, 6, d), lambda bi, i: (bi, 0, 0)),
            _const_spec((1, d)),
            _const_spec((d, inw)),
            _const_spec((1, LANES)),
            _const_spec((1, LANES)),
            tab, tab, tab,
            _const_spec((LANES, LANES)),
            _const_spec((N_SG_GROUPS, CHUNK, CHUNK)),
            _const_spec((N_SG_GROUPS, CHUNK, LANES)),
        ],
        out_specs=[
            pl.BlockSpec((None, npair, TM, LANES), lambda bi, i: (bi, 0, i, 0)),
            pl.BlockSpec((None, TM, kvw), row),
            pl.BlockSpec((None, TM, kvw), row),
            pl.BlockSpec((None, TM, sgw), row),
        ],
        out_shape=[
            jax.ShapeDtypeStruct((b, npair, s, LANES), BF16),
            jax.ShapeDtypeStruct((b, s, kvw), BF16),
            jax.ShapeDtypeStruct((b, s, kvw), BF16),
            jax.ShapeDtypeStruct((b, s, sgw), BF16),
        ],
        compiler_params=_params(2),
        name="l0_pre",
    )(x, mod, g_mix, w_in, qg2, kg2, cos, sup, sdn, bd, w_sp, bsp)


def _ctx_kv_kernel(x_ref, mod_ref, g_ref, w_ref, kg_ref, bd_ref, k_ref, v_ref):
    kvw = N_KV_HEADS * HEAD_DIM
    xm = _rms_mod(x_ref[...], g_ref[...], mod_ref[1:2, :], mod_ref[0:1, :]).astype(BF16)
    p = _dot(xm, w_ref[...])
    k_ref[...] = _head_rms(p[:, 0:kvw], bd_ref[...], kg_ref[...]).astype(BF16)
    v_ref[...] = p[:, kvw:2 * kvw].astype(BF16)


def _ctx_kv(ctx, cmod, g_mix, w_kv, kg2, bd):
    b, m, d = ctx.shape
    kvw = N_KV_HEADS * HEAD_DIM
    row = lambda bi: (bi, 0, 0)
    return pl.pallas_call(
        _ctx_kv_kernel,
        grid=(b,),
        in_specs=[
            pl.BlockSpec((None, m, d), row),
            _const_spec((6, d)),
            _const_spec((1, d)),
            _const_spec((d, 2 * kvw)),
            _const_spec((1, LANES)),
            _const_spec((LANES, LANES)),
        ],
        out_specs=[pl.BlockSpec((None, m, kvw), row), pl.BlockSpec((None, m, kvw), row)],
        out_shape=[jax.ShapeDtypeStruct((b, m, kvw), BF16), jax.ShapeDtypeStruct((b, m, kvw), BF16)],
        compiler_params=_params(1),
        name="ctx_kv",
    )(ctx, cmod, g_mix, w_kv, kg2, bd)


def _attn_kernel(q_ref, k_ref, kc_ref, v_ref, vc_ref, o_ref, ka_scr, kb_scr, vx_scr, m_scr, acc_scr):
    h = pl.program_id(1)
    i = pl.program_id(2)
    tq = q_ref.shape[1]
    n_ctx = kc_ref.shape[0]
    n_lat = k_ref.shape[0]
    rb = n_ctx

    @pl.when(i == 0)
    def _fill():
        lane = lax.broadcasted_iota(jnp.int32, (rb, LANES), 1)
        lo = lane < HEAD_DIM
        one = jnp.where(lane == HEAD_DIM, 1.0, 0.0)

        def put(kk, vv, r0):
            kk = kk.astype(F32)
            vv = vv.astype(F32)
            ksw = pltpu.roll(kk, HEAD_DIM, 1)
            vsw = pltpu.roll(vv, HEAD_DIM, 1)
            k_lo = jnp.where(h == 0, kk, ksw)
            k_hi = jnp.where(h == 0, ksw, kk)
            v_lo = jnp.where(h == 0, vv, vsw)
            ka_scr[pl.ds(r0, rb), :] = jnp.where(lo, k_lo, 0.0).astype(BF16)
            kb_scr[pl.ds(r0, rb), :] = jnp.where(lo, 0.0, k_hi).astype(BF16)
            vx_scr[pl.ds(r0, rb), :] = jnp.where(lo, v_lo, one).astype(BF16)

        put(kc_ref[...], vc_ref[...], 0)

        def body(t, carry):
            r0 = pl.multiple_of(t * rb, rb)
            put(k_ref[pl.ds(r0, rb), :], v_ref[pl.ds(r0, rb), :], n_ctx + r0)
            return carry

        lax.fori_loop(0, n_lat // rb, body, 0)

    qs = q_ref[...].reshape(2 * tq, LANES)
    nt = (((1,), (1,)), ((), ()))

    def chunk(r0, n, first):
        ka = ka_scr[pl.ds(r0, n), :]
        kb = kb_scr[pl.ds(r0, n), :]
        vx = vx_scr[pl.ds(r0, n), :]
        for half, kk in enumerate((ka, kb)):
            s = lax.dot_general(qs, kk, nt, preferred_element_type=F32)
            rows = pl.ds(half * 2 * tq, 2 * tq)
            mc = s[:, 0:LANES]
            for c in range(1, n // LANES):
                mc = jnp.maximum(mc, s[:, c * LANES:(c + 1) * LANES])
            mc = jnp.max(mc, axis=-1, keepdims=True)
            if first:
                m_new = jnp.broadcast_to(mc, (2 * tq, LANES))
            else:
                m_prev = m_scr[rows, :]
                m_new = jnp.maximum(m_prev, mc)
            p = jnp.concatenate(
                [jnp.exp(s[:, c * LANES:(c + 1) * LANES] - m_new).astype(BF16) for c in range(n // LANES)],
                axis=1)
            pv = _dot(p, vx)
            if first:
                acc_scr[rows, :] = pv
            else:
                acc_scr[rows, :] = jnp.exp(m_prev - m_new) * acc_scr[rows, :] + pv
            m_scr[rows, :] = m_new

    chunk(0, n_ctx, True)

    def body(t, carry):
        chunk(pl.multiple_of(n_ctx + t * TK, LANES), TK, False)
        return carry

    lax.fori_loop(0, n_lat // TK, body, 0)

    acc = acc_scr[...]
    lane = lax.broadcasted_iota(jnp.int32, acc.shape, 1)
    denom = jnp.sum(jnp.where(lane == HEAD_DIM, acc, 0.0), axis=-1, keepdims=True)
    o = acc / denom
    lo = lax.broadcasted_iota(jnp.int32, (tq, LANES), 1) < HEAD_DIM
    o_ref[:, 0:LANES] = jnp.where(lo, o[0:tq], pltpu.roll(o[2 * tq:3 * tq], HEAD_DIM, 1)).astype(BF16)
    o_ref[:, LANES:2 * LANES] = jnp.where(lo, o[tq:2 * tq], pltpu.roll(o[3 * tq:4 * tq], HEAD_DIM, 1)).astype(BF16)


def _attention(q, k, kc, v, vc):
    b, npair, s, _ = q.shape
    m = kc.shape[1]
    kvw = k.shape[2]
    pairs_per_kv = npair // N_KV_HEADS
    gw = pairs_per_kv * LANES
    full = lambda bi, h, i: (bi, 0, 0)
    return pl.pallas_call(
        _attn_kernel,
        grid=(b, N_KV_HEADS, s // TQ),
        in_specs=[
            pl.BlockSpec((None, pairs_per_kv, TQ, LANES), lambda bi, h, i: (bi, h, i, 0)),
            pl.BlockSpec((None, s, kvw), full),
            pl.BlockSpec((None, m, kvw), full),
            pl.BlockSpec((None, s, kvw), full),
            pl.BlockSpec((None, m, kvw), full),
        ],
        out_specs=pl.BlockSpec((None, TQ, gw), lambda bi, h, i: (bi, i, h)),
        out_shape=jax.ShapeDtypeStruct((b, s, N_KV_HEADS * gw), BF16),
        scratch_shapes=[
            pltpu.VMEM((m + s, LANES), BF16),
            pltpu.VMEM((m + s, LANES), BF16),
            pltpu.VMEM((m + s, LANES), BF16),
            pltpu.VMEM((2 * pairs_per_kv * TQ, LANES), F32),
            pltpu.VMEM((2 * pairs_per_kv * TQ, LANES), F32),
        ],
        compiler_params=_params(3),
        name="attn",
    )(q, k, kc, v, vc)


def _ffn_residual(h, mod_ref, g_ref, wg_ref, wu_ref, wo_ref, xm_scr, acc_scr):
    xm_scr[...] = _rms_mod(h, g_ref[...], mod_ref[4:5, :], mod_ref[3:4, :]).astype(BF16)
    acc_scr[...] = jnp.zeros(acc_scr.shape, F32)

    def body(c, carry):
        xm = xm_scr[...]
        g = _dot(xm, wg_ref[c])
        u = _dot(xm, wu_ref[c])
        a = (g * _sigmoid(g) * u).astype(BF16)
        acc_scr[...] += _dot(a, wo_ref[c])
        return carry

    lax.fori_loop(0, wg_ref.shape[0], body, 0)
    return h + mod_ref[5:6, :] * acc_scr[...]


def _ffn_specs(d, nck):
    return [_const_spec((1, d)), _const_spec((nck, d, FFN_CHUNK)), _const_spec((nck, d, FFN_CHUNK)),
            _const_spec((nck, FFN_CHUNK, d))]


def _ffn_scratch(d):
    return [pltpu.VMEM((TM, d), BF16), pltpu.VMEM((TM, d), F32)]


def _l0_post_kernel(x_ref, a_ref, sg_ref, mod_ref, wo_ref, gf_ref, wg_ref, wu_ref, wd_ref, o_ref, xm_scr, acc_scr):
    aw = a_ref.shape[1]
    y = _dot(a_ref[...], wo_ref[0:aw, :]) + _dot(sg_ref[...], wo_ref[aw:, :])
    h = x_ref[...] + mod_ref[2:3, :] * y
    o_ref[...] = _ffn_residual(h, mod_ref, gf_ref, wg_ref, wu_ref, wd_ref, xm_scr, acc_scr)


def _l0_post(x, attn, sg, mod, w_out, g_ffn, wg, wu, wd):
    b, s, d = x.shape
    row = lambda bi, i: (bi, i, 0)
    return pl.pallas_call(
        _l0_post_kernel,
        grid=(b, s // TM),
        in_specs=[
            pl.BlockSpec((None, TM, d), row),
            pl.BlockSpec((None, TM, attn.shape[2]), row),
            pl.BlockSpec((None, TM, sg.shape[2]), row),
            pl.BlockSpec((None, 6, d), lambda bi, i: (bi, 0, 0)),
            _const_spec(w_out.shape),
        ] + _ffn_specs(d, wg.shape[0]),
        out_specs=pl.BlockSpec((None, TM, d), row),
        out_shape=jax.ShapeDtypeStruct((b, s, d), F32),
        scratch_shapes=_ffn_scratch(d),
        compiler_params=_params(2),
        name="l0_post",
    )(x, attn, sg, mod, w_out, g_ffn, wg, wu, wd)


def _l1_glu_kernel(h_ref, mod_ref, g_ref, w_ref, b_ref, o_ref):
    d = h_ref.shape[1]
    xm = _rms_mod(h_ref[...], g_ref[...], mod_ref[1:2, :], mod_ref[0:1, :]).astype(BF16)
    z = _dot(xm, w_ref[...]) + b_ref[...]
    o_ref[...] = z[:, 0:d] * _sigmoid(z[:, d:2 * d])


def _l1_glu(h, mod, g_mix, w_pw1, b_pw1):
    b, s, d = h.shape
    row = lambda bi, i: (bi, i, 0)
    return pl.pallas_call(
        _l1_glu_kernel,
        grid=(b, s // TM),
        in_specs=[
            pl.BlockSpec((None, TM, d), row),
            pl.BlockSpec((None, 6, d), lambda bi, i: (bi, 0, 0)),
            _const_spec((1, d)),
            _const_spec(w_pw1.shape),
            _const_spec((1, 2 * d)),
        ],
        out_specs=pl.BlockSpec((None, TM, d), row),
        out_shape=jax.ShapeDtypeStruct((b, s, d), F32),
        compiler_params=_params(2),
        name="l1_glu",
    )(h, mod, g_mix, w_pw1, b_pw1)


def _l1_post_kernel(h_ref, u_ref, up_ref, un_ref, mod_ref, wdw_ref, bdw_ref, lng_ref, lnb_ref, w2_ref, b2_ref,
                    gf_ref, wg_ref, wu_ref, wd_ref, gfin_ref, o_ref, e_scr, c_scr, xm_scr, acc_scr):
    i = pl.program_id(1)
    n_i = pl.num_programs(1)
    tm = h_ref.shape[0]
    nslab = e_scr.shape[0]
    up = jnp.where(i > 0, up_ref[...], 0.0)
    un = jnp.where(i < n_i - 1, un_ref[...], 0.0)
    for sl in range(nslab):
        cols = slice(sl * LANES, (sl + 1) * LANES)
        e_scr[sl, 0:HALO, :] = up[:, cols]
        e_scr[sl, HALO:HALO + tm, :] = u_ref[:, cols]
        e_scr[sl, HALO + tm:, :] = un[:, cols]
    rb = 64
    off = HALO - CONV_WIDTH // 2

    def conv_slab(sl, carry):
        for r0 in range(0, tm, rb):
            acc = jnp.broadcast_to(bdw_ref[sl], (rb, LANES))
            for j in range(CONV_WIDTH):
                acc = acc + wdw_ref[sl, j:j + 1, :] * e_scr[sl, r0 + off + j:r0 + off + j + rb, :]
            c_scr[sl, r0:r0 + rb, :] = acc
        return carry

    lax.fori_loop(0, nslab, conv_slab, 0)
    cv = jnp.concatenate([c_scr[sl] for sl in range(nslab)], axis=1)
    mu = jnp.mean(cv, axis=-1, keepdims=True)
    dv = cv - mu
    var = jnp.mean(dv * dv, axis=-1, keepdims=True)
    ln = dv * lax.rsqrt(var + EPS) * lng_ref[...] + lnb_ref[...]
    act = (ln * _sigmoid(ln)).astype(BF16)
    y = _dot(act, w2_ref[...]) + b2_ref[...]
    h = h_ref[...] + mod_ref[2:3, :] * y
    h = _ffn_residual(h, mod_ref, gf_ref, wg_ref, wu_ref, wd_ref, xm_scr, acc_scr)
    ms = jnp.mean(h * h, axis=-1, keepdims=True)
    o_ref[...] = h * lax.rsqrt(ms + EPS) * gfin_ref[...]


def _l1_post(h, u, mod, w_dw, b_dw, ln_g, ln_b, w_pw2, b_pw2, g_ffn, wg, wu, wd, g_final):
    b, s, d = h.shape
    row = lambda bi, i: (bi, i, 0)
    per = TM // HALO
    last = s // HALO - 1
    vec = _const_spec((1, d))
    return pl.pallas_call(
        _l1_post_kernel,
        grid=(b, s // TM),
        in_specs=[
            pl.BlockSpec((None, TM, d), row),
            pl.BlockSpec((None, TM, d), row),
            pl.BlockSpec((None, HALO, d), lambda bi, i: (bi, jnp.maximum(i * per - 1, 0), 0)),
            pl.BlockSpec((None, HALO, d), lambda bi, i: (bi, jnp.minimum((i + 1) * per, last), 0)),
            pl.BlockSpec((None, 6, d), lambda bi, i: (bi, 0, 0)),
            _const_spec(w_dw.shape), _const_spec(b_dw.shape), vec, vec,
            _const_spec(w_pw2.shape), vec,
        ] + _ffn_specs(d, wg.shape[0]) + [vec],
        out_specs=pl.BlockSpec((None, TM, d), row),
        out_shape=jax.ShapeDtypeStruct((b, s, d), F32),
        scratch_shapes=[pltpu.VMEM((d // LANES, TM + 2 * HALO, LANES), F32),
                        pltpu.VMEM((d // LANES, TM, LANES), F32)] + _ffn_scratch(d),
        compiler_params=_params(2),
        name="l1_post",
    )(h, u, u, u, mod, w_dw, b_dw, ln_g, ln_b, w_pw2, b_pw2, g_ffn, wg, wu, wd, g_final)


def _rope_tables(n):
    half = HEAD_DIM // 2
    pos = jnp.arange(n)
    row = (pos // GRID_W).astype(F32)
    col = (pos % GRID_W).astype(F32)
    inv = ROPE_THETA ** (-jnp.arange(0, half, 2, dtype=F32) / half)
    lane = jnp.arange(LANES)
    hd = lane % HEAD_DIM
    w = hd % half
    ang = jnp.where((hd < half)[None, :], row[:, None], col[:, None]) * inv[w % (half // 2)][None, :]
    first = (w < half // 2)[None, :]
    sin = jnp.sin(ang)
    return jnp.cos(ang), jnp.where(first, -sin, 0.0), jnp.where(first, 0.0, sin)


def _ffn_weights(w_in, w_out):
    d, two_ff = w_in.shape
    dff = two_ff // 2
    nck = dff // FFN_CHUNK
    wg = w_in[:, :dff].astype(BF16).reshape(d, nck, FFN_CHUNK).transpose(1, 0, 2)
    wu = w_in[:, dff:].astype(BF16).reshape(d, nck, FFN_CHUNK).transpose(1, 0, 2)
    wd = w_out.astype(BF16).reshape(nck, FFN_CHUNK, d)
    return wg, wu, wd


def kernel(x, c, ctx, c_ctx, w_mod, b_mod, g_mix, g_ffn, w_ffn_in, w_ffn_out, w_in, q_gain, k_gain, w_sp, b_sp,
           w_out, w_pw1, b_pw1, w_dw, b_dw, ln_g, ln_b, w_pw2, b_pw2, g_final):
    b, s, d = x.shape
    aw = N_Q_HEADS * HEAD_DIM
    kvw = N_KV_HEADS * HEAD_DIM

    rows = -(-(b + 1) // 8) * 8
    cvec = jnp.concatenate([c, c_ctx[None, :], jnp.zeros((rows - b - 1, d), F32)], axis=0)
    mod = _adaln(cvec, w_mod, b_mod)
    mod0 = mod[0, :b].reshape(b, 6, d)
    cmod0 = mod[0, b].reshape(6, d)
    mod1 = mod[1, :b].reshape(b, 6, d)

    cos, sup, sdn = _rope_tables(s)
    lane = jnp.arange(LANES)
    bd = (lane[:, None] // HEAD_DIM == lane[None, :] // HEAD_DIM).astype(BF16)
    qg2 = jnp.tile(q_gain[0], LANES // HEAD_DIM)[None, :]
    kg2 = jnp.tile(k_gain[0], LANES // HEAD_DIM)[None, :]
    w_in_b = w_in[0].astype(BF16)
    bsp = jnp.broadcast_to(b_sp[0][:, :, None], (N_SG_GROUPS, CHUNK, LANES))

    q, k, v, sg = _l0_pre(x, mod0, g_mix[0:1], w_in_b, qg2, kg2, cos, sup, sdn, bd, w_sp[0].astype(BF16), bsp)
    kc, vc = _ctx_kv(ctx, cmod0, g_mix[0:1], w_in_b[:, aw:aw + 2 * kvw], kg2, bd)
    attn = _attention(q, k, kc, v, vc)

    wg0, wu0, wd0 = _ffn_weights(w_ffn_in[0], w_ffn_out[0])
    h = _l0_post(x, attn, sg, mod0, w_out[0].astype(BF16), g_ffn[0:1], wg0, wu0, wd0)

    u = _l1_glu(h, mod1, g_mix[1:2], w_pw1[0].astype(BF16), b_pw1[0:1])
    wg1, wu1, wd1 = _ffn_weights(w_ffn_in[1], w_ffn_out[1])
    w_dw_s = w_dw[0].reshape(CONV_WIDTH, d // LANES, LANES).transpose(1, 0, 2)
    b_dw_s = b_dw[0].reshape(d // LANES, 1, LANES)
    return _l1_post(h, u, mod1, w_dw_s, b_dw_s, ln_g[0:1], ln_b[0:1], w_pw2[0].astype(BF16), b_pw2[0:1],
                    g_ffn[1:2], wg1, wu1, wd1, g_final[None, :])
```

```python
import functools

import jax
import jax.numpy as jnp
import numpy as np
from jax import lax
from jax.experimental import pallas as pl
from jax.experimental.pallas import tpu as pltpu

F32 = jnp.float32
BF16 = jnp.bfloat16

HEAD_DIM = 64
N_Q_HEADS = 8
N_KV_HEADS = 2
GRID_W = 64
CHUNK = 128
N_SG_GROUPS = 4
ROPE_THETA = 10000.0
CONV_WIDTH = 31
EPS = 1e-6
Q_SCALE = HEAD_DIM ** -0.5 * 1.4426950408889634

LANES = 128
HALO = 16
FFN_CHUNK = 256
VMEM_LIMIT = 56 * 1024 * 1024

TM = 512
TQ = 512
TK = 512


def _const_spec(shape):
    nd = len(shape)
    return pl.BlockSpec(shape, lambda *_: (0,) * nd, pipeline_mode=pl.Buffered(1))


def _params(n_axes, flags=None):
    return pltpu.CompilerParams(dimension_semantics=("arbitrary",) * n_axes,
                                vmem_limit_bytes=VMEM_LIMIT, flags=flags)


def _dot(a, b):
    return jnp.dot(a, b, preferred_element_type=F32)


def _sigmoid(x):
    return 1.0 / (1.0 + jnp.exp(-x))


def _gelu_tanh(x):
    c = 0.7978845608028654
    return x * (0.5 * (1.0 + jnp.tanh(c * (x + 0.044715 * (x * x * x)))))


def _rms_mod(x, g, sc, sh):
    ms = jnp.mean(x * x, axis=-1, keepdims=True)
    return (x * lax.rsqrt(ms + EPS) * g) * (1.0 + sc) + sh


def _head_rms(z, bd, gain):
    z2 = z * z
    hi = z2.astype(BF16)
    lo = (z2 - hi.astype(F32)).astype(BF16)
    ssq = _dot(hi, bd) + _dot(lo, bd)
    return z * lax.rsqrt(ssq * (1.0 / HEAD_DIM) + EPS) * gain


def _rope(z, cos, s_up, s_dn):
    return z * cos + pltpu.roll(z, LANES - 16, 1) * s_up + pltpu.roll(z, 16, 1) * s_dn


def _adaln_kernel(c_ref, w_ref, b_ref, o_ref):
    c = c_ref[...]
    s = (c * _sigmoid(c)).astype(BF16)
    o_ref[...] = _dot(s, w_ref[...].astype(BF16)) + b_ref[...]


def _adaln(cvec, w_mod, b_mod):
    depth, d, n = w_mod.shape
    rows = cvec.shape[0]
    tn = 1536
    return pl.pallas_call(
        _adaln_kernel,
        grid=(depth, n // tn),
        in_specs=[
            pl.BlockSpec((rows, d), lambda l, j: (0, 0)),
            pl.BlockSpec((None, d, tn), lambda l, j: (l, 0, j)),
            pl.BlockSpec((None, 1, tn), lambda l, j: (l, 0, j)),
        ],
        out_specs=pl.BlockSpec((None, rows, tn), lambda l, j: (l, 0, j)),
        out_shape=jax.ShapeDtypeStruct((depth, rows, n), F32),
        compiler_params=_params(2),
        name="adaln",
    )(cvec, w_mod, b_mod.reshape(depth, 1, n))


def _l0_pre_kernel(x_ref, mod_ref, g_ref, w_ref, qg_ref, kg_ref, cos_ref, sup_ref, sdn_ref, bd_ref,
                   wsp_ref, bsp_ref, q_ref, k_ref, v_ref, sg_ref):
    tm = x_ref.shape[0]
    aw = N_Q_HEADS * HEAD_DIM
    kvw = N_KV_HEADS * HEAD_DIM
    sgw = N_SG_GROUPS * LANES
    xm = _rms_mod(x_ref[...], g_ref[...], mod_ref[1:2, :], mod_ref[0:1, :]).astype(BF16)
    p = _dot(xm, w_ref[...])
    bd = bd_ref[...]
    cos, sup, sdn = cos_ref[...], sup_ref[...], sdn_ref[...]
    for j in range(aw // LANES):
        qn = _head_rms(p[:, j * LANES:(j + 1) * LANES], bd, qg_ref[...])
        q_ref[j] = (_rope(qn, cos, sup, sdn) * Q_SCALE).T.astype(BF16)
    kn = _head_rms(p[:, aw:aw + kvw], bd, kg_ref[...])
    k_ref[...] = _rope(kn, cos, sup, sdn).astype(BF16)
    v_ref[...] = p[:, aw + kvw:aw + 2 * kvw].T.astype(BF16)
    o_u = aw + 2 * kvw
    o_v = o_u + sgw
    nchunk = tm // CHUNK
    for g in range(N_SG_GROUPS):
        gv = _gelu_tanh(p[:, o_v + g * LANES:o_v + (g + 1) * LANES])
        mu = jnp.mean(gv, axis=-1, keepdims=True)
        dv = gv - mu
        var = jnp.mean(dv * dv, axis=-1, keepdims=True)
        vn = (dv * lax.rsqrt(var + EPS)).astype(BF16)
        rhs = jnp.concatenate([vn[c * CHUNK:(c + 1) * CHUNK, :] for c in range(nchunk)], axis=1)
        mixed = _dot(wsp_ref[g], rhs)
        gu = _gelu_tanh(p[:, o_u + g * LANES:o_u + (g + 1) * LANES])
        for c in range(nchunk):
            blk = mixed[:, c * CHUNK:(c + 1) * CHUNK] + bsp_ref[g]
            sg_ref[c * CHUNK:(c + 1) * CHUNK, g * LANES:(g + 1) * LANES] = (
                gu[c * CHUNK:(c + 1) * CHUNK, :] * blk).astype(BF16)


def _l0_pre(x, mod, g_mix, w_in, qg2, kg2, cos, sup, sdn, bd, w_sp, bsp):
    b, s, d = x.shape
    inw = w_in.shape[1]
    npair = N_Q_HEADS * HEAD_DIM // LANES
    kvw = N_KV_HEADS * HEAD_DIM
    sgw = N_SG_GROUPS * LANES
    row = lambda bi, i: (bi, i, 0)
    tab = pl.BlockSpec((TM, LANES), lambda bi, i: (i, 0))
    return pl.pallas_call(
        _l0_pre_kernel,
        grid=(b, s // TM),
        in_specs=[
            pl.BlockSpec((None, TM, d), row),
            pl.BlockSpec((None, 6, d), lambda bi, i: (bi, 0, 0)),
            _const_spec((1, d)),
            _const_spec((d, inw)),
            _const_spec((1, LANES)),
            _const_spec((1, LANES)),
            tab, tab, tab,
            _const_spec((LANES, LANES)),
            _const_spec((N_SG_GROUPS, CHUNK, CHUNK)),
            _const_spec((N_SG_GROUPS, CHUNK, LANES)),
        ],
        out_specs=[
            pl.BlockSpec((None, npair, LANES, TM), lambda bi, i: (bi, 0, 0, i)),
            pl.BlockSpec((None, TM, kvw), row),
            pl.BlockSpec((None, kvw, TM), lambda bi, i: (bi, 0, i)),
            pl.BlockSpec((None, TM, sgw), row),
        ],
        out_shape=[
            jax.ShapeDtypeStruct((b, npair, LANES, s), BF16),
            jax.ShapeDtypeStruct((b, s, kvw), BF16),
            jax.ShapeDtypeStruct((b, kvw, s), BF16),
            jax.ShapeDtypeStruct((b, s, sgw), BF16),
        ],
        compiler_params=_params(2),
        name="l0_pre",
    )(x, mod, g_mix, w_in, qg2, kg2, cos, sup, sdn, bd, w_sp, bsp)


def _ctx_kv_kernel(x_ref, mod_ref, g_ref, w_ref, kg_ref, bd_ref, k_ref, v_ref):
    kvw = N_KV_HEADS * HEAD_DIM
    xm = _rms_mod(x_ref[...], g_ref[...], mod_ref[1:2, :], mod_ref[0:1, :]).astype(BF16)
    p = _dot(xm, w_ref[...])
    k_ref[...] = _head_rms(p[:, 0:kvw], bd_ref[...], kg_ref[...]).astype(BF16)
    v_ref[...] = p[:, kvw:2 * kvw].T.astype(BF16)


def _ctx_kv(ctx, cmod, g_mix, w_kv, kg2, bd):
    b, m, d = ctx.shape
    kvw = N_KV_HEADS * HEAD_DIM
    row = lambda bi: (bi, 0, 0)
    return pl.pallas_call(
        _ctx_kv_kernel,
        grid=(b,),
        in_specs=[
            pl.BlockSpec((None, m, d), row),
            _const_spec((6, d)),
            _const_spec((1, d)),
            _const_spec((d, 2 * kvw)),
            _const_spec((1, LANES)),
            _const_spec((LANES, LANES)),
        ],
        out_specs=[pl.BlockSpec((None, m, kvw), row), pl.BlockSpec((None, kvw, m), row)],
        out_shape=[jax.ShapeDtypeStruct((b, m, kvw), BF16), jax.ShapeDtypeStruct((b, kvw, m), BF16)],
        compiler_params=_params(1),
        name="ctx_kv",
    )(ctx, cmod, g_mix, w_kv, kg2, bd)


NEG_INIT = -1e30
VX_ROWS = HEAD_DIM + 16


def _attn_kernel(q_ref, k_ref, kc_ref, v_ref, vc_ref, o_ref, ka_scr, kb_scr, vx_scr, s_scr, mx_scr, m_scr, acc_scr):
    h = pl.program_id(1)
    i = pl.program_id(2)
    tq = q_ref.shape[2]
    n_ctx = kc_ref.shape[0]
    n_keys = ka_scr.shape[0]
    chunks = [(r0, min(TK, n_ctx - r0)) for r0 in range(0, n_ctx, TK)]
    chunks += [(r0, TK) for r0 in range(n_ctx, n_keys, TK)]

    @pl.when(i == 0)
    def _fill():
        hrow = pl.ds(pl.multiple_of(h * HEAD_DIM, HEAD_DIM), HEAD_DIM)

        def put_k(kk, r0):
            lo = lax.broadcasted_iota(jnp.int32, kk.shape, 1) < HEAD_DIM
            kk = kk.astype(F32)
            ksw = pltpu.roll(kk, HEAD_DIM, 1)
            k_lo = jnp.where(h == 0, kk, ksw)
            k_hi = jnp.where(h == 0, ksw, kk)
            ka_scr[r0:r0 + kk.shape[0], :] = jnp.where(lo, k_lo, 0.0).astype(BF16)
            kb_scr[r0:r0 + kk.shape[0], :] = jnp.where(lo, 0.0, k_hi).astype(BF16)

        for r0, nk in chunks:
            if r0 < n_ctx:
                put_k(kc_ref[r0:r0 + nk, :], r0)
                vx_scr[0:HEAD_DIM, r0:r0 + nk] = vc_ref[hrow, r0:r0 + nk]
            else:
                put_k(k_ref[r0 - n_ctx:r0 - n_ctx + nk, :], r0)
                vx_scr[0:HEAD_DIM, r0:r0 + nk] = v_ref[hrow, r0 - n_ctx:r0 - n_ctx + nk]
            ones_row = lax.broadcasted_iota(jnp.int32, (VX_ROWS - HEAD_DIM, nk), 0) == 0
            vx_scr[HEAD_DIM:, r0:r0 + nk] = jnp.where(ones_row, 1.0, 0.0).astype(BF16)

    m_scr[...] = jnp.full(m_scr.shape, NEG_INIT, F32)
    acc_scr[...] = jnp.zeros(acc_scr.shape, F32)
    heads = range(m_scr.shape[0])

    def scores(c):
        r0, nk = chunks[c]
        for g in heads:
            k_scr = kb_scr if g % 2 else ka_scr
            s = _dot(k_scr[r0:r0 + nk, :], q_ref[g // 2])
            s_scr[c % 2, g, 0:nk, :] = s
            mx_scr[c % 2, g] = jnp.max(s, axis=0, keepdims=True)

    def softmax_pv(c):
        r0, nk = chunks[c]
        for g in heads:
            m_old = m_scr[g]
            m_new = jnp.maximum(m_old, mx_scr[c % 2, g])
            p = jnp.exp2(s_scr[c % 2, g, 0:nk, :] - m_new).astype(BF16)
            acc_scr[g] = jnp.exp2(m_old - m_new) * acc_scr[g] + _dot(vx_scr[:, r0:r0 + nk], p)
            m_scr[g] = m_new

    scores(0)
    for c in range(len(chunks)):
        if c + 1 < len(chunks):
            scores(c + 1)
        softmax_pv(c)

    for pair in range(len(heads) // 2):
        a = acc_scr[2 * pair]
        b = acc_scr[2 * pair + 1]
        st = jnp.concatenate([a[0:HEAD_DIM] / a[HEAD_DIM:HEAD_DIM + 1], b[0:HEAD_DIM] / b[HEAD_DIM:HEAD_DIM + 1]],
                             axis=0)
        o_ref[:, pair * LANES:(pair + 1) * LANES] = st.T.astype(BF16)


def _attention(q, k, kc, v, vc):
    b, npair, _, s = q.shape
    m = kc.shape[1]
    kvw = k.shape[2]
    pairs_per_kv = npair // N_KV_HEADS
    gw = pairs_per_kv * LANES
    assert s % TK == 0 and m % LANES == 0, (m, s, TK)
    full = lambda bi, h, i: (bi, 0, 0)
    return pl.pallas_call(
        _attn_kernel,
        grid=(b, N_KV_HEADS, s // TQ),
        in_specs=[
            pl.BlockSpec((None, pairs_per_kv, LANES, TQ), lambda bi, h, i: (bi, h, 0, i)),
            pl.BlockSpec((None, s, kvw), full),
            pl.BlockSpec((None, m, kvw), full),
            pl.BlockSpec((None, kvw, s), full),
            pl.BlockSpec((None, kvw, m), full),
        ],
        out_specs=pl.BlockSpec((None, TQ, gw), lambda bi, h, i: (bi, i, h)),
        out_shape=jax.ShapeDtypeStruct((b, s, N_KV_HEADS * gw), BF16),
        scratch_shapes=[
            pltpu.VMEM((m + s, LANES), BF16),
            pltpu.VMEM((m + s, LANES), BF16),
            pltpu.VMEM((VX_ROWS, m + s), BF16),
            pltpu.VMEM((2, 2 * pairs_per_kv, TK, TQ), F32),
            pltpu.VMEM((2, 2 * pairs_per_kv, 1, TQ), F32),
            pltpu.VMEM((2 * pairs_per_kv, 1, TQ), F32),
            pltpu.VMEM((2 * pairs_per_kv, VX_ROWS, TQ), F32),
        ],
        compiler_params=_params(3),
        name="attn",
    )(q, k, kc, v, vc)


def _ffn_residual(h, mod_ref, g_ref, wi_ref, wo_ref, xm_scr, acc_scr):
    dff = wo_ref.shape[0]
    xm_scr[...] = _rms_mod(h, g_ref[...], mod_ref[4:5, :], mod_ref[3:4, :]).astype(BF16)
    acc_scr[...] = jnp.zeros(acc_scr.shape, F32)

    for c0 in range(0, dff, FFN_CHUNK):
        xm = xm_scr[...]
        g = _dot(xm, wi_ref[:, c0:c0 + FFN_CHUNK])
        u = _dot(xm, wi_ref[:, dff + c0:dff + c0 + FFN_CHUNK])
        a = (g * _sigmoid(g) * u).astype(BF16)
        acc_scr[...] += _dot(a, wo_ref[c0:c0 + FFN_CHUNK, :])
    return h + mod_ref[5:6, :] * acc_scr[...]


def _ffn_specs(d, dff):
    return [_const_spec((1, d)), _const_spec((d, 2 * dff)), _const_spec((dff, d))]


def _ffn_scratch(d):
    return [pltpu.VMEM((TM, d), BF16), pltpu.VMEM((TM, d), F32)]


def _l0_post_kernel(x_ref, a_ref, sg_ref, mod_ref, wo_ref, gf_ref, wi_ref, wd_ref, o_ref, xm_scr, acc_scr):
    aw = a_ref.shape[1]
    y = _dot(a_ref[...], wo_ref[0:aw, :]) + _dot(sg_ref[...], wo_ref[aw:, :])
    h = x_ref[...] + mod_ref[2:3, :] * y
    o_ref[...] = _ffn_residual(h, mod_ref, gf_ref, wi_ref, wd_ref, xm_scr, acc_scr)


def _l0_post(x, attn, sg, mod, w_out, g_ffn, wi, wd):
    b, s, d = x.shape
    row = lambda bi, i: (bi, i, 0)
    return pl.pallas_call(
        _l0_post_kernel,
        grid=(b, s // TM),
        in_specs=[
            pl.BlockSpec((None, TM, d), row),
            pl.BlockSpec((None, TM, attn.shape[2]), row),
            pl.BlockSpec((None, TM, sg.shape[2]), row),
            pl.BlockSpec((None, 6, d), lambda bi, i: (bi, 0, 0)),
            _const_spec(w_out.shape),
        ] + _ffn_specs(d, wd.shape[0]),
        out_specs=pl.BlockSpec((None, TM, d), row),
        out_shape=jax.ShapeDtypeStruct((b, s, d), F32),
        scratch_shapes=_ffn_scratch(d),
        compiler_params=_params(2),
        name="l0_post",
    )(x, attn, sg, mod, w_out, g_ffn, wi, wd)


def _l1_glu_kernel(h_ref, mod_ref, g_ref, w_ref, b_ref, o_ref):
    d = h_ref.shape[1]
    xm = _rms_mod(h_ref[...], g_ref[...], mod_ref[1:2, :], mod_ref[0:1, :]).astype(BF16)
    z = _dot(xm, w_ref[...]) + b_ref[...]
    o_ref[...] = z[:, 0:d] * _sigmoid(z[:, d:2 * d])


def _l1_glu(h, mod, g_mix, w_pw1, b_pw1):
    b, s, d = h.shape
    row = lambda bi, i: (bi, i, 0)
    return pl.pallas_call(
        _l1_glu_kernel,
        grid=(b, s // TM),
        in_specs=[
            pl.BlockSpec((None, TM, d), row),
            pl.BlockSpec((None, 6, d), lambda bi, i: (bi, 0, 0)),
            _const_spec((1, d)),
            _const_spec(w_pw1.shape),
            _const_spec((1, 2 * d)),
        ],
        out_specs=pl.BlockSpec((None, TM, d), row),
        out_shape=jax.ShapeDtypeStruct((b, s, d), F32),
        compiler_params=_params(2),
        name="l1_glu",
    )(h, mod, g_mix, w_pw1, b_pw1)


def _l1_post_kernel(h_ref, u_ref, up_ref, un_ref, mod_ref, wdw_ref, bdw_ref, lng_ref, lnb_ref, w2_ref, b2_ref,
                    gf_ref, wi_ref, wd_ref, gfin_ref, o_ref, e_scr, c_scr, xm_scr, acc_scr):
    i = pl.program_id(1)
    n_i = pl.num_programs(1)
    tm = h_ref.shape[0]
    nslab = e_scr.shape[0]
    up = jnp.where(i > 0, up_ref[...], 0.0)
    un = jnp.where(i < n_i - 1, un_ref[...], 0.0)
    for sl in range(nslab):
        cols = slice(sl * LANES, (sl + 1) * LANES)
        e_scr[sl, 0:HALO, :] = up[:, cols]
        e_scr[sl, HALO:HALO + tm, :] = u_ref[:, cols]
        e_scr[sl, HALO + tm:, :] = un[:, cols]
    rb = 64
    off = HALO - CONV_WIDTH // 2

    def conv_slab(sl, carry):
        for r0 in range(0, tm, rb):
            acc = jnp.broadcast_to(bdw_ref[sl], (rb, LANES))
            for j in range(CONV_WIDTH):
                acc = acc + wdw_ref[sl, j:j + 1, :] * e_scr[sl, r0 + off + j:r0 + off + j + rb, :]
            c_scr[sl, r0:r0 + rb, :] = acc
        return carry

    lax.fori_loop(0, nslab, conv_slab, 0)
    cv = jnp.concatenate([c_scr[sl] for sl in range(nslab)], axis=1)
    mu = jnp.mean(cv, axis=-1, keepdims=True)
    dv = cv - mu
    var = jnp.mean(dv * dv, axis=-1, keepdims=True)
    ln = dv * lax.rsqrt(var + EPS) * lng_ref[...] + lnb_ref[...]
    act = (ln * _sigmoid(ln)).astype(BF16)
    y = _dot(act, w2_ref[...]) + b2_ref[...]
    h = h_ref[...] + mod_ref[2:3, :] * y
    h = _ffn_residual(h, mod_ref, gf_ref, wi_ref, wd_ref, xm_scr, acc_scr)
    ms = jnp.mean(h * h, axis=-1, keepdims=True)
    o_ref[...] = h * lax.rsqrt(ms + EPS) * gfin_ref[...]


def _l1_post(h, u, mod, w_dw, b_dw, ln_g, ln_b, w_pw2, b_pw2, g_ffn, wi, wd, g_final):
    b, s, d = h.shape
    row = lambda bi, i: (bi, i, 0)
    per = TM // HALO
    last = s // HALO - 1
    vec = _const_spec((1, d))
    return pl.pallas_call(
        _l1_post_kernel,
        grid=(b, s // TM),
        in_specs=[
            pl.BlockSpec((None, TM, d), row),
            pl.BlockSpec((None, TM, d), row),
            pl.BlockSpec((None, HALO, d), lambda bi, i: (bi, jnp.maximum(i * per - 1, 0), 0)),
            pl.BlockSpec((None, HALO, d), lambda bi, i: (bi, jnp.minimum((i + 1) * per, last), 0)),
            pl.BlockSpec((None, 6, d), lambda bi, i: (bi, 0, 0)),
            _const_spec(w_dw.shape), _const_spec(b_dw.shape), vec, vec,
            _const_spec(w_pw2.shape), vec,
        ] + _ffn_specs(d, wd.shape[0]) + [vec],
        out_specs=pl.BlockSpec((None, TM, d), row),
        out_shape=jax.ShapeDtypeStruct((b, s, d), F32),
        scratch_shapes=[pltpu.VMEM((d // LANES, TM + 2 * HALO, LANES), F32),
                        pltpu.VMEM((d // LANES, TM, LANES), F32)] + _ffn_scratch(d),
        compiler_params=_params(2),
        name="l1_post",
    )(h, u, u, u, mod, w_dw, b_dw, ln_g, ln_b, w_pw2, b_pw2, g_ffn, wi, wd, g_final)


def _rope_tables(n):
    half = HEAD_DIM // 2
    pos = np.arange(n)
    row = (pos // GRID_W).astype(np.float32)
    col = (pos % GRID_W).astype(np.float32)
    inv = (ROPE_THETA ** (-np.arange(0, half, 2, dtype=np.float32) / half)).astype(np.float32)
    lane = np.arange(LANES)
    hd = lane % HEAD_DIM
    w = hd % half
    ang = np.where((hd < half)[None, :], row[:, None], col[:, None]) * inv[w % (half // 2)][None, :]
    first = (w < half // 2)[None, :]
    sin = np.sin(ang)
    tabs = (np.cos(ang), np.where(first, -sin, 0.0), np.where(first, 0.0, sin))
    return tuple(jnp.asarray(t, F32) for t in tabs)


def kernel(x, c, ctx, c_ctx, w_mod, b_mod, g_mix, g_ffn, w_ffn_in, w_ffn_out, w_in, q_gain, k_gain, w_sp, b_sp,
           w_out, w_pw1, b_pw1, w_dw, b_dw, ln_g, ln_b, w_pw2, b_pw2, g_final):
    b, s, d = x.shape
    aw = N_Q_HEADS * HEAD_DIM
    kvw = N_KV_HEADS * HEAD_DIM

    rows = -(-(b + 1) // 8) * 8
    cvec = jnp.concatenate([c, c_ctx[None, :], jnp.zeros((rows - b - 1, d), F32)], axis=0)
    mod = _adaln(cvec, w_mod, b_mod)
    mod0 = mod[0, :b].reshape(b, 6, d)
    cmod0 = mod[0, b].reshape(6, d)
    mod1 = mod[1, :b].reshape(b, 6, d)

    cos, sup, sdn = _rope_tables(s)
    lane = jnp.arange(LANES)
    bd = (lane[:, None] // HEAD_DIM == lane[None, :] // HEAD_DIM).astype(BF16)
    qg2 = jnp.tile(q_gain[0], LANES // HEAD_DIM)[None, :]
    kg2 = jnp.tile(k_gain[0], LANES // HEAD_DIM)[None, :]
    w_in_b = w_in[0].astype(BF16)
    bsp = jnp.broadcast_to(b_sp[0][:, :, None], (N_SG_GROUPS, CHUNK, LANES))

    q, k, v, sg = _l0_pre(x, mod0, g_mix[0:1], w_in_b, qg2, kg2, cos, sup, sdn, bd, w_sp[0].astype(BF16), bsp)
    kc, vc = _ctx_kv(ctx, cmod0, g_mix[0:1], w_in_b[:, aw:aw + 2 * kvw], kg2, bd)
    attn = _attention(q, k, kc, v, vc)

    assert w_ffn_out.shape[1] % FFN_CHUNK == 0, w_ffn_out.shape
    wi_b = w_ffn_in.astype(BF16)
    wd_b = w_ffn_out.astype(BF16)
    h = _l0_post(x, attn, sg, mod0, w_out[0].astype(BF16), g_ffn[0:1], wi_b[0], wd_b[0])

    u = _l1_glu(h, mod1, g_mix[1:2], w_pw1[0].astype(BF16), b_pw1[0:1])
    w_dw_s = w_dw[0].reshape(CONV_WIDTH, d // LANES, LANES).transpose(1, 0, 2)
    b_dw_s = b_dw[0].reshape(d // LANES, 1, LANES)
    return _l1_post(h, u, mod1, w_dw_s, b_dw_s, ln_g[0:1], ln_b[0:1], w_pw2[0].astype(BF16), b_pw2[0:1],
                    g_ffn[1:2], wi_b[1], wd_b[1], g_final[None, :])
```

```python
import functools

import jax
import jax.numpy as jnp
import numpy as np
from jax import lax
from jax.experimental import pallas as pl
from jax.experimental.pallas import tpu as pltpu

F32 = jnp.float32
BF16 = jnp.bfloat16

HEAD_DIM = 64
N_Q_HEADS = 8
N_KV_HEADS = 2
GRID_W = 64
CHUNK = 128
N_SG_GROUPS = 4
ROPE_THETA = 10000.0
CONV_WIDTH = 31
EPS = 1e-6
Q_SCALE = HEAD_DIM ** -0.5 * 1.4426950408889634

LANES = 128
HALO = 16
FFN_CHUNK = 256
VMEM_LIMIT = 56 * 1024 * 1024

TM = 512
TQ = 512
TK = 256


def _const_spec(shape):
    nd = len(shape)
    return pl.BlockSpec(shape, lambda *_: (0,) * nd, pipeline_mode=pl.Buffered(1))


def _params(n_axes, flags=None):
    return pltpu.CompilerParams(dimension_semantics=("arbitrary",) * n_axes,
                                vmem_limit_bytes=VMEM_LIMIT, flags=flags)


def _dot(a, b):
    return jnp.dot(a, b, preferred_element_type=F32)


def _sigmoid(x):
    return 1.0 / (1.0 + jnp.exp(-x))


def _gelu_tanh(x):
    c = 0.7978845608028654
    return x * (0.5 * (1.0 + jnp.tanh(c * (x + 0.044715 * (x * x * x)))))


def _rms_mod(x, g, sc, sh):
    ms = jnp.mean(x * x, axis=-1, keepdims=True)
    return (x * lax.rsqrt(ms + EPS) * g) * (1.0 + sc) + sh


def _head_rms(z, bd, gain):
    z2 = z * z
    hi = z2.astype(BF16)
    lo = (z2 - hi.astype(F32)).astype(BF16)
    ssq = _dot(hi, bd) + _dot(lo, bd)
    return z * lax.rsqrt(ssq * (1.0 / HEAD_DIM) + EPS) * gain


def _rope(z, cos, s_up, s_dn):
    return z * cos + pltpu.roll(z, LANES - 16, 1) * s_up + pltpu.roll(z, 16, 1) * s_dn


def _adaln_kernel(c_ref, w_ref, b_ref, o_ref):
    c = c_ref[...]
    s = (c * _sigmoid(c)).astype(BF16)
    o_ref[...] = _dot(s, w_ref[...].astype(BF16)) + b_ref[...]


def _adaln(cvec, w_mod, b_mod):
    depth, d, n = w_mod.shape
    rows = cvec.shape[0]
    tn = 1536
    return pl.pallas_call(
        _adaln_kernel,
        grid=(depth, n // tn),
        in_specs=[
            pl.BlockSpec((rows, d), lambda l, j: (0, 0)),
            pl.BlockSpec((None, d, tn), lambda l, j: (l, 0, j)),
            pl.BlockSpec((None, 1, tn), lambda l, j: (l, 0, j)),
        ],
        out_specs=pl.BlockSpec((None, rows, tn), lambda l, j: (l, 0, j)),
        out_shape=jax.ShapeDtypeStruct((depth, rows, n), F32),
        compiler_params=_params(2),
        name="adaln",
    )(cvec, w_mod, b_mod.reshape(depth, 1, n))


def _l0_pre_kernel(x_ref, mod_ref, g_ref, w_ref, qg_ref, kg_ref, cos_ref, sup_ref, sdn_ref, bd_ref,
                   wsp_ref, bsp_ref, q_ref, k_ref, v_ref, sg_ref):
    aw = N_Q_HEADS * HEAD_DIM
    kvw = N_KV_HEADS * HEAD_DIM
    sgw = N_SG_GROUPS * LANES
    o_u = aw + 2 * kvw
    o_v = o_u + sgw
    bd = bd_ref[...]
    tm = x_ref.shape[0]
    xm = _rms_mod(x_ref[...], g_ref[...], mod_ref[1:2, :], mod_ref[0:1, :]).astype(BF16)
    p = _dot(xm, w_ref[...])
    cos, sup, sdn = cos_ref[...], sup_ref[...], sdn_ref[...]
    for j in range(aw // LANES):
        qn = _head_rms(p[:, j * LANES:(j + 1) * LANES], bd, qg_ref[...])
        q_ref[j] = (_rope(qn, cos, sup, sdn) * Q_SCALE).T.astype(BF16)
    kn = _head_rms(p[:, aw:aw + kvw], bd, kg_ref[...])
    k_ref[...] = _rope(kn, cos, sup, sdn).astype(BF16)
    v_ref[...] = p[:, aw + kvw:aw + 2 * kvw].T.astype(BF16)
    nchunk = tm // CHUNK
    for g in range(N_SG_GROUPS):
        gv = _gelu_tanh(p[:, o_v + g * LANES:o_v + (g + 1) * LANES])
        mu = jnp.mean(gv, axis=-1, keepdims=True)
        dv = gv - mu
        var = jnp.mean(dv * dv, axis=-1, keepdims=True)
        vn = (dv * lax.rsqrt(var + EPS)).astype(BF16)
        rhs = jnp.concatenate([vn[c * CHUNK:(c + 1) * CHUNK, :] for c in range(nchunk)], axis=1)
        mixed = _dot(wsp_ref[g], rhs)
        gu = _gelu_tanh(p[:, o_u + g * LANES:o_u + (g + 1) * LANES])
        for c in range(nchunk):
            blk = mixed[:, c * CHUNK:(c + 1) * CHUNK] + bsp_ref[g]
            sg_ref[c * CHUNK:(c + 1) * CHUNK, g * LANES:(g + 1) * LANES] = (
                gu[c * CHUNK:(c + 1) * CHUNK, :] * blk).astype(BF16)


def _l0_pre(x, mod, g_mix, w_in, qg2, kg2, cos, sup, sdn, bd, w_sp, bsp):
    b, s, d = x.shape
    inw = w_in.shape[1]
    npair = N_Q_HEADS * HEAD_DIM // LANES
    kvw = N_KV_HEADS * HEAD_DIM
    sgw = N_SG_GROUPS * LANES
    row = lambda bi, i: (bi, i, 0)
    tab = pl.BlockSpec((TM, LANES), lambda bi, i: (i, 0))
    return pl.pallas_call(
        _l0_pre_kernel,
        grid=(b, s // TM),
        in_specs=[
            pl.BlockSpec((None, TM, d), row),
            pl.BlockSpec((None, 6, d), lambda bi, i: (bi, 0, 0)),
            _const_spec((1, d)),
            _const_spec((d, inw)),
            _const_spec((1, LANES)),
            _const_spec((1, LANES)),
            tab, tab, tab,
            _const_spec((LANES, LANES)),
            _const_spec((N_SG_GROUPS, CHUNK, CHUNK)),
            _const_spec((N_SG_GROUPS, CHUNK, LANES)),
        ],
        out_specs=[
            pl.BlockSpec((None, npair, LANES, TM), lambda bi, i: (bi, 0, 0, i)),
            pl.BlockSpec((None, TM, kvw), row),
            pl.BlockSpec((None, kvw, TM), lambda bi, i: (bi, 0, i)),
            pl.BlockSpec((None, TM, sgw), row),
        ],
        out_shape=[
            jax.ShapeDtypeStruct((b, npair, LANES, s), BF16),
            jax.ShapeDtypeStruct((b, s, kvw), BF16),
            jax.ShapeDtypeStruct((b, kvw, s), BF16),
            jax.ShapeDtypeStruct((b, s, sgw), BF16),
        ],
        compiler_params=_params(2),
        name="l0_pre",
    )(x, mod, g_mix, w_in, qg2, kg2, cos, sup, sdn, bd, w_sp, bsp)


def _ctx_kv_kernel(x_ref, mod_ref, g_ref, w_ref, kg_ref, bd_ref, k_ref, v_ref):
    kvw = N_KV_HEADS * HEAD_DIM
    xm = _rms_mod(x_ref[...], g_ref[...], mod_ref[1:2, :], mod_ref[0:1, :]).astype(BF16)
    p = _dot(xm, w_ref[...])
    k_ref[...] = _head_rms(p[:, 0:kvw], bd_ref[...], kg_ref[...]).astype(BF16)
    v_ref[...] = p[:, kvw:2 * kvw].T.astype(BF16)


def _ctx_kv(ctx, cmod, g_mix, w_kv, kg2, bd):
    b, m, d = ctx.shape
    kvw = N_KV_HEADS * HEAD_DIM
    row = lambda bi: (bi, 0, 0)
    return pl.pallas_call(
        _ctx_kv_kernel,
        grid=(b,),
        in_specs=[
            pl.BlockSpec((None, m, d), row),
            _const_spec((6, d)),
            _const_spec((1, d)),
            _const_spec((d, 2 * kvw)),
            _const_spec((1, LANES)),
            _const_spec((LANES, LANES)),
        ],
        out_specs=[pl.BlockSpec((None, m, kvw), row), pl.BlockSpec((None, kvw, m), row)],
        out_shape=[jax.ShapeDtypeStruct((b, m, kvw), BF16), jax.ShapeDtypeStruct((b, kvw, m), BF16)],
        compiler_params=_params(1),
        name="ctx_kv",
    )(ctx, cmod, g_mix, w_kv, kg2, bd)


NEG_INIT = -1e30
VX_ROWS = HEAD_DIM + 16


def _attn_kernel(q_ref, k_ref, kc_ref, v_ref, vc_ref, o_ref, ka_scr, kb_scr, vx_scr, s_scr, mx_scr, m_scr, acc_scr):
    h = pl.program_id(1)
    i = pl.program_id(2)
    tq = q_ref.shape[2]
    n_ctx = kc_ref.shape[0]
    n_keys = ka_scr.shape[0]
    chunks = [(r0, min(TK, n_ctx - r0)) for r0 in range(0, n_ctx, TK)]
    chunks += [(r0, TK) for r0 in range(n_ctx, n_keys, TK)]

    @pl.when(i == 0)
    def _fill():
        hrow = pl.ds(pl.multiple_of(h * HEAD_DIM, HEAD_DIM), HEAD_DIM)

        def put_k(kk, r0):
            lo = lax.broadcasted_iota(jnp.int32, kk.shape, 1) < HEAD_DIM
            kk = kk.astype(F32)
            ksw = pltpu.roll(kk, HEAD_DIM, 1)
            k_lo = jnp.where(h == 0, kk, ksw)
            k_hi = jnp.where(h == 0, ksw, kk)
            ka_scr[r0:r0 + kk.shape[0], :] = jnp.where(lo, k_lo, 0.0).astype(BF16)
            kb_scr[r0:r0 + kk.shape[0], :] = jnp.where(lo, 0.0, k_hi).astype(BF16)

        for r0, nk in chunks:
            if r0 < n_ctx:
                put_k(kc_ref[r0:r0 + nk, :], r0)
                vx_scr[0:HEAD_DIM, r0:r0 + nk] = vc_ref[hrow, r0:r0 + nk]
            else:
                put_k(k_ref[r0 - n_ctx:r0 - n_ctx + nk, :], r0)
                vx_scr[0:HEAD_DIM, r0:r0 + nk] = v_ref[hrow, r0 - n_ctx:r0 - n_ctx + nk]
            ones_row = lax.broadcasted_iota(jnp.int32, (VX_ROWS - HEAD_DIM, nk), 0) == 0
            vx_scr[HEAD_DIM:, r0:r0 + nk] = jnp.where(ones_row, 1.0, 0.0).astype(BF16)

    m_scr[...] = jnp.full(m_scr.shape, NEG_INIT, F32)
    acc_scr[...] = jnp.zeros(acc_scr.shape, F32)
    heads = range(m_scr.shape[0])

    def scores(c, g):
        r0, nk = chunks[c]
        k_scr = kb_scr if g % 2 else ka_scr
        s = _dot(k_scr[r0:r0 + nk, :], q_ref[g // 2])
        s_scr[c % 2, g, 0:nk, :] = s
        mx_scr[c % 2, g] = jnp.max(s, axis=0, keepdims=True)

    def softmax_pv(c, g):
        r0, nk = chunks[c]
        m_old = m_scr[g]
        m_new = jnp.maximum(m_old, mx_scr[c % 2, g])
        p = jnp.exp2(s_scr[c % 2, g, 0:nk, :] - m_new).astype(BF16)
        acc_scr[g] = jnp.exp2(m_old - m_new) * acc_scr[g] + _dot(vx_scr[:, r0:r0 + nk], p)
        m_scr[g] = m_new

    for g in heads:
        scores(0, g)
    for c in range(len(chunks)):
        for g in heads:
            if c + 1 < len(chunks):
                scores(c + 1, g)
            softmax_pv(c, g)

    for pair in range(len(heads) // 2):
        a = acc_scr[2 * pair]
        b = acc_scr[2 * pair + 1]
        st = jnp.concatenate([a[0:HEAD_DIM] / a[HEAD_DIM:HEAD_DIM + 1], b[0:HEAD_DIM] / b[HEAD_DIM:HEAD_DIM + 1]],
                             axis=0)
        o_ref[:, pair * LANES:(pair + 1) * LANES] = st.T.astype(BF16)


def _attention(q, k, kc, v, vc):
    b, npair, _, s = q.shape
    m = kc.shape[1]
    kvw = k.shape[2]
    pairs_per_kv = npair // N_KV_HEADS
    gw = pairs_per_kv * LANES
    assert s % TK == 0 and m % LANES == 0, (m, s, TK)
    full = lambda bi, h, i: (bi, 0, 0)
    return pl.pallas_call(
        _attn_kernel,
        grid=(b, N_KV_HEADS, s // TQ),
        in_specs=[
            pl.BlockSpec((None, pairs_per_kv, LANES, TQ), lambda bi, h, i: (bi, h, 0, i)),
            pl.BlockSpec((None, s, kvw), full),
            pl.BlockSpec((None, m, kvw), full),
            pl.BlockSpec((None, kvw, s), full),
            pl.BlockSpec((None, kvw, m), full),
        ],
        out_specs=pl.BlockSpec((None, TQ, gw), lambda bi, h, i: (bi, i, h)),
        out_shape=jax.ShapeDtypeStruct((b, s, N_KV_HEADS * gw), BF16),
        scratch_shapes=[
            pltpu.VMEM((m + s, LANES), BF16),
            pltpu.VMEM((m + s, LANES), BF16),
            pltpu.VMEM((VX_ROWS, m + s), BF16),
            pltpu.VMEM((2, 2 * pairs_per_kv, TK, TQ), F32),
            pltpu.VMEM((2, 2 * pairs_per_kv, 1, TQ), F32),
            pltpu.VMEM((2 * pairs_per_kv, 1, TQ), F32),
            pltpu.VMEM((2 * pairs_per_kv, VX_ROWS, TQ), F32),
        ],
        compiler_params=_params(3),
        name="attn",
    )(q, k, kc, v, vc)


def _ffn_residual(h, mod_ref, g_ref, wi_ref, wo_ref, xm_scr, acc_scr):
    dff = wo_ref.shape[0]
    xm_scr[...] = _rms_mod(h, g_ref[...], mod_ref[4:5, :], mod_ref[3:4, :]).astype(BF16)
    acc_scr[...] = jnp.zeros(acc_scr.shape, F32)

    for c0 in range(0, dff, FFN_CHUNK):
        xm = xm_scr[...]
        g = _dot(xm, wi_ref[:, c0:c0 + FFN_CHUNK])
        u = _dot(xm, wi_ref[:, dff + c0:dff + c0 + FFN_CHUNK])
        a = (g * _sigmoid(g) * u).astype(BF16)
        acc_scr[...] += _dot(a, wo_ref[c0:c0 + FFN_CHUNK, :])
    return h + mod_ref[5:6, :] * acc_scr[...]


def _layer_spec(w, layer):
    return pl.BlockSpec((None,) + w.shape[1:], lambda *_: (layer, 0, 0), pipeline_mode=pl.Buffered(1))


def _ffn_specs(d, wi, wd, layer):
    return [_const_spec((1, d)), _layer_spec(wi, layer), _layer_spec(wd, layer)]


def _ffn_scratch(d):
    return [pltpu.VMEM((TM, d), BF16), pltpu.VMEM((TM, d), F32)]


def _l0_post_kernel(x_ref, a_ref, sg_ref, mod_ref, wo_ref, gf_ref, wi_ref, wd_ref, mod1_ref, g1_ref, w1_ref, b1_ref,
                    h_ref, u_ref, xm_scr, acc_scr):
    aw = a_ref.shape[1]
    d = x_ref.shape[1]
    y = _dot(a_ref[...], wo_ref[0:aw, :]) + _dot(sg_ref[...], wo_ref[aw:, :])
    h = x_ref[...] + mod_ref[2:3, :] * y
    h = _ffn_residual(h, mod_ref, gf_ref, wi_ref, wd_ref, xm_scr, acc_scr)
    h_ref[...] = h
    xm_scr[...] = _rms_mod(h, g1_ref[...], mod1_ref[1:2, :], mod1_ref[0:1, :]).astype(BF16)
    for c0 in range(0, d, FFN_CHUNK):
        xm = xm_scr[...]
        a = _dot(xm, w1_ref[:, c0:c0 + FFN_CHUNK]) + b1_ref[:, c0:c0 + FFN_CHUNK]
        gate = _dot(xm, w1_ref[:, d + c0:d + c0 + FFN_CHUNK]) + b1_ref[:, d + c0:d + c0 + FFN_CHUNK]
        u_ref[:, c0:c0 + FFN_CHUNK] = a * _sigmoid(gate)


def _l0_post(x, attn, sg, mod, w_out, g_ffn, wi, wd, mod1, g_mix1, w_pw1, b_pw1):
    b, s, d = x.shape
    row = lambda bi, i: (bi, i, 0)
    mods = pl.BlockSpec((None, 6, d), lambda bi, i: (bi, 0, 0))
    tile = pl.BlockSpec((None, TM, d), row)
    return pl.pallas_call(
        _l0_post_kernel,
        grid=(b, s // TM),
        in_specs=[
            tile,
            pl.BlockSpec((None, TM, attn.shape[2]), row),
            pl.BlockSpec((None, TM, sg.shape[2]), row),
            mods,
            _const_spec(w_out.shape),
        ] + _ffn_specs(d, wi, wd, 0) + [mods, _const_spec((1, d)), _const_spec(w_pw1.shape), _const_spec((1, 2 * d))],
        out_specs=[tile, tile],
        out_shape=[jax.ShapeDtypeStruct((b, s, d), F32), jax.ShapeDtypeStruct((b, s, d), F32)],
        scratch_shapes=_ffn_scratch(d),
        compiler_params=_params(2),
        name="l0_post",
    )(x, attn, sg, mod, w_out, g_ffn, wi, wd, mod1, g_mix1, w_pw1, b_pw1)


def _l1_post_kernel(h_ref, u_ref, up_ref, un_ref, mod_ref, wdw_ref, bdw_ref, lng_ref, lnb_ref, w2_ref, b2_ref,
                    gf_ref, wi_ref, wd_ref, gfin_ref, o_ref, e_scr, c_scr, xm_scr, acc_scr):
    i = pl.program_id(1)
    n_i = pl.num_programs(1)
    tm = h_ref.shape[0]
    nslab = e_scr.shape[0]
    up = jnp.where(i > 0, up_ref[...], 0.0)
    un = jnp.where(i < n_i - 1, un_ref[...], 0.0)
    for sl in range(nslab):
        cols = slice(sl * LANES, (sl + 1) * LANES)
        e_scr[sl, 0:HALO, :] = up[:, cols]
        e_scr[sl, HALO:HALO + tm, :] = u_ref[:, cols]
        e_scr[sl, HALO + tm:, :] = un[:, cols]
    rb = 64
    off = HALO - CONV_WIDTH // 2

    def conv_slab(sl, carry):
        for r0 in range(0, tm, rb):
            acc = jnp.broadcast_to(bdw_ref[sl], (rb, LANES))
            for j in range(CONV_WIDTH):
                acc = acc + wdw_ref[sl, j:j + 1, :] * e_scr[sl, r0 + off + j:r0 + off + j + rb, :]
            c_scr[sl, r0:r0 + rb, :] = acc
        return carry

    lax.fori_loop(0, nslab, conv_slab, 0)
    cv = jnp.concatenate([c_scr[sl] for sl in range(nslab)], axis=1)
    mu = jnp.mean(cv, axis=-1, keepdims=True)
    dv = cv - mu
    var = jnp.mean(dv * dv, axis=-1, keepdims=True)
    ln = dv * lax.rsqrt(var + EPS) * lng_ref[...] + lnb_ref[...]
    act = (ln * _sigmoid(ln)).astype(BF16)
    y = _dot(act, w2_ref[...]) + b2_ref[...]
    h = h_ref[...] + mod_ref[2:3, :] * y
    h = _ffn_residual(h, mod_ref, gf_ref, wi_ref, wd_ref, xm_scr, acc_scr)
    ms = jnp.mean(h * h, axis=-1, keepdims=True)
    o_ref[...] = h * lax.rsqrt(ms + EPS) * gfin_ref[...]


def _l1_post(h, u, mod, w_dw, b_dw, ln_g, ln_b, w_pw2, b_pw2, g_ffn, wi, wd, g_final):
    b, s, d = h.shape
    row = lambda bi, i: (bi, i, 0)
    per = TM // HALO
    last = s // HALO - 1
    vec = _const_spec((1, d))
    return pl.pallas_call(
        _l1_post_kernel,
        grid=(b, s // TM),
        in_specs=[
            pl.BlockSpec((None, TM, d), row),
            pl.BlockSpec((None, TM, d), row),
            pl.BlockSpec((None, HALO, d), lambda bi, i: (bi, jnp.maximum(i * per - 1, 0), 0)),
            pl.BlockSpec((None, HALO, d), lambda bi, i: (bi, jnp.minimum((i + 1) * per, last), 0)),
            pl.BlockSpec((None, 6, d), lambda bi, i: (bi, 0, 0)),
            _const_spec(w_dw.shape), _const_spec(b_dw.shape), vec, vec,
            _const_spec(w_pw2.shape), vec,
        ] + _ffn_specs(d, wi, wd, 1) + [vec],
        out_specs=pl.BlockSpec((None, TM, d), row),
        out_shape=jax.ShapeDtypeStruct((b, s, d), F32),
        scratch_shapes=[pltpu.VMEM((d // LANES, TM + 2 * HALO, LANES), F32),
                        pltpu.VMEM((d // LANES, TM, LANES), F32)] + _ffn_scratch(d),
        compiler_params=_params(2),
        name="l1_post",
    )(h, u, u, u, mod, w_dw, b_dw, ln_g, ln_b, w_pw2, b_pw2, g_ffn, wi, wd, g_final)


def _rope_tables(n):
    half = HEAD_DIM // 2
    pos = np.arange(n)
    row = (pos // GRID_W).astype(np.float32)
    col = (pos % GRID_W).astype(np.float32)
    inv = (ROPE_THETA ** (-np.arange(0, half, 2, dtype=np.float32) / half)).astype(np.float32)
    lane = np.arange(LANES)
    hd = lane % HEAD_DIM
    w = hd % half
    ang = np.where((hd < half)[None, :], row[:, None], col[:, None]) * inv[w % (half // 2)][None, :]
    first = (w < half // 2)[None, :]
    sin = np.sin(ang)
    tabs = (np.cos(ang), np.where(first, -sin, 0.0), np.where(first, 0.0, sin))
    return tuple(jnp.asarray(t, F32) for t in tabs)


def kernel(x, c, ctx, c_ctx, w_mod, b_mod, g_mix, g_ffn, w_ffn_in, w_ffn_out, w_in, q_gain, k_gain, w_sp, b_sp,
           w_out, w_pw1, b_pw1, w_dw, b_dw, ln_g, ln_b, w_pw2, b_pw2, g_final):
    b, s, d = x.shape
    aw = N_Q_HEADS * HEAD_DIM
    kvw = N_KV_HEADS * HEAD_DIM

    rows = -(-(b + 1) // 8) * 8
    cvec = jnp.concatenate([c, c_ctx[None, :], jnp.zeros((rows - b - 1, d), F32)], axis=0)
    mod = _adaln(cvec, w_mod, b_mod)
    mod0 = mod[0, :b].reshape(b, 6, d)
    cmod0 = mod[0, b].reshape(6, d)
    mod1 = mod[1, :b].reshape(b, 6, d)

    cos, sup, sdn = _rope_tables(s)
    lane = jnp.arange(LANES)
    bd = (lane[:, None] // HEAD_DIM == lane[None, :] // HEAD_DIM).astype(BF16)
    qg2 = jnp.tile(q_gain[0], LANES // HEAD_DIM)[None, :]
    kg2 = jnp.tile(k_gain[0], LANES // HEAD_DIM)[None, :]
    w_in_b = w_in[0].astype(BF16)
    bsp = jnp.broadcast_to(b_sp[0][:, :, None], (N_SG_GROUPS, CHUNK, LANES))

    q, k, v, sg = _l0_pre(x, mod0, g_mix[0:1], w_in_b, qg2, kg2, cos, sup, sdn, bd, w_sp[0].astype(BF16), bsp)
    kc, vc = _ctx_kv(ctx, cmod0, g_mix[0:1], w_in_b[:, aw:aw + 2 * kvw], kg2, bd)
    attn = _attention(q, k, kc, v, vc)

    assert w_ffn_out.shape[1] % FFN_CHUNK == 0 and d % FFN_CHUNK == 0, (w_ffn_out.shape, d)
    wi_b = w_ffn_in.astype(BF16)
    wd_b = w_ffn_out.astype(BF16)
    h, u = _l0_post(x, attn, sg, mod0, w_out[0].astype(BF16), g_ffn[0:1], wi_b, wd_b,
                    mod1, g_mix[1:2], w_pw1[0].astype(BF16), b_pw1[0:1])
    w_dw_s = w_dw[0].reshape(CONV_WIDTH, d // LANES, LANES).transpose(1, 0, 2)
    b_dw_s = b_dw[0].reshape(d // LANES, 1, LANES)
    return _l1_post(h, u, mod1, w_dw_s, b_dw_s, ln_g[0:1], ln_b[0:1], w_pw2[0].astype(BF16), b_pw2[0:1],
                    g_ffn[1:2], wi_b, wd_b, g_final[None, :])
```

```python
import functools

import jax
import jax.numpy as jnp
import numpy as np
from jax import lax
from jax.experimental import pallas as pl
from jax.experimental.pallas import tpu as pltpu

F32 = jnp.float32
BF16 = jnp.bfloat16

HEAD_DIM = 64
N_Q_HEADS = 8
N_KV_HEADS = 2
GRID_W = 64
CHUNK = 128
N_SG_GROUPS = 4
ROPE_THETA = 10000.0
CONV_WIDTH = 31
EPS = 1e-6
Q_SCALE = HEAD_DIM ** -0.5 * 1.4426950408889634

LANES = 128
HALO = 16
FFN_CHUNK = 256
VMEM_LIMIT = 56 * 1024 * 1024

TM = 512
TQ = 512
TK = 256


def _const_spec(shape):
    nd = len(shape)
    return pl.BlockSpec(shape, lambda *_: (0,) * nd, pipeline_mode=pl.Buffered(1))


def _params(n_axes, flags=None):
    return pltpu.CompilerParams(dimension_semantics=("arbitrary",) * n_axes,
                                vmem_limit_bytes=VMEM_LIMIT, flags=flags)


def _dot(a, b):
    return jnp.dot(a, b, preferred_element_type=F32)


def _sigmoid(x):
    return 1.0 / (1.0 + jnp.exp(-x))


def _gelu_tanh(x):
    c = 0.7978845608028654
    return x * (0.5 * (1.0 + jnp.tanh(c * (x + 0.044715 * (x * x * x)))))


def _rms_mod(x, g, sc, sh):
    ms = jnp.mean(x * x, axis=-1, keepdims=True)
    return (x * lax.rsqrt(ms + EPS) * g) * (1.0 + sc) + sh


def _head_rms(z, bd, gain):
    z2 = z * z
    hi = z2.astype(BF16)
    lo = (z2 - hi.astype(F32)).astype(BF16)
    ssq = _dot(hi, bd) + _dot(lo, bd)
    return z * lax.rsqrt(ssq * (1.0 / HEAD_DIM) + EPS) * gain


def _rope(z, cos, s_up, s_dn):
    return z * cos + pltpu.roll(z, LANES - 16, 1) * s_up + pltpu.roll(z, 16, 1) * s_dn


def _adaln_kernel(c_ref, w_ref, b_ref, o_ref):
    c = c_ref[...]
    s = (c * _sigmoid(c)).astype(BF16)
    o_ref[...] = _dot(s, w_ref[...].astype(BF16)) + b_ref[...]


def _adaln(cvec, w_mod, b_mod):
    depth, d, n = w_mod.shape
    rows = cvec.shape[0]
    tn = 1536
    return pl.pallas_call(
        _adaln_kernel,
        grid=(depth, n // tn),
        in_specs=[
            pl.BlockSpec((rows, d), lambda l, j: (0, 0)),
            pl.BlockSpec((None, d, tn), lambda l, j: (l, 0, j)),
            pl.BlockSpec((None, 1, tn), lambda l, j: (l, 0, j)),
        ],
        out_specs=pl.BlockSpec((None, rows, tn), lambda l, j: (l, 0, j)),
        out_shape=jax.ShapeDtypeStruct((depth, rows, n), F32),
        compiler_params=_params(2),
        name="adaln",
    )(cvec, w_mod, b_mod.reshape(depth, 1, n))


def _l0_pre_kernel(x_ref, mod_ref, g_ref, w_ref, qg_ref, kg_ref, cos_ref, sup_ref, sdn_ref, bd_ref,
                   wsp_ref, bsp_ref, q_ref, k_ref, v_ref, sg_ref, pa_scr, pb_scr):
    i = pl.program_id(1)
    aw = N_Q_HEADS * HEAD_DIM
    kvw = N_KV_HEADS * HEAD_DIM
    sgw = N_SG_GROUPS * LANES
    o_u = aw + 2 * kvw
    o_v = o_u + sgw
    tm = x_ref.shape[0]
    nchunk = tm // CHUNK

    @pl.when(i == 0)
    def _zero():
        pb_scr[...] = jnp.zeros(pb_scr.shape, F32)

    def step(p_new, p_old):
        xm = _rms_mod(x_ref[...], g_ref[...], mod_ref[1:2, :], mod_ref[0:1, :]).astype(BF16)
        p_new[...] = _dot(xm, w_ref[...])
        bd = bd_ref[...]
        cos, sup, sdn = cos_ref[...], sup_ref[...], sdn_ref[...]
        for j in range(aw // LANES):
            qn = _head_rms(p_old[:, j * LANES:(j + 1) * LANES], bd, qg_ref[...])
            q_ref[j] = (_rope(qn, cos, sup, sdn) * Q_SCALE).T.astype(BF16)
        kn = _head_rms(p_old[:, aw:aw + kvw], bd, kg_ref[...])
        k_ref[...] = _rope(kn, cos, sup, sdn).astype(BF16)
        v_ref[...] = p_old[:, aw + kvw:aw + 2 * kvw].T.astype(BF16)
        for g in range(N_SG_GROUPS):
            gv = _gelu_tanh(p_old[:, o_v + g * LANES:o_v + (g + 1) * LANES])
            mu = jnp.mean(gv, axis=-1, keepdims=True)
            dv = gv - mu
            var = jnp.mean(dv * dv, axis=-1, keepdims=True)
            vn = (dv * lax.rsqrt(var + EPS)).astype(BF16)
            rhs = jnp.concatenate([vn[c * CHUNK:(c + 1) * CHUNK, :] for c in range(nchunk)], axis=1)
            mixed = _dot(wsp_ref[g], rhs)
            gu = _gelu_tanh(p_old[:, o_u + g * LANES:o_u + (g + 1) * LANES])
            for c in range(nchunk):
                blk = mixed[:, c * CHUNK:(c + 1) * CHUNK] + bsp_ref[g]
                sg_ref[c * CHUNK:(c + 1) * CHUNK, g * LANES:(g + 1) * LANES] = (
                    gu[c * CHUNK:(c + 1) * CHUNK, :] * blk).astype(BF16)

    @pl.when(lax.rem(i, 2) == 0)
    def _even():
        step(pa_scr, pb_scr)

    @pl.when(lax.rem(i, 2) == 1)
    def _odd():
        step(pb_scr, pa_scr)


def _l0_pre(x, mod, g_mix, w_in, qg2, kg2, cos, sup, sdn, bd, w_sp, bsp):
    b, s, d = x.shape
    inw = w_in.shape[1]
    npair = N_Q_HEADS * HEAD_DIM // LANES
    kvw = N_KV_HEADS * HEAD_DIM
    sgw = N_SG_GROUPS * LANES
    n_i = s // TM
    prev = lambda i: jnp.maximum(i - 1, 0)
    row = lambda bi, i: (bi, prev(i), 0)
    tab = pl.BlockSpec((TM, LANES), lambda bi, i: (prev(i), 0))
    return pl.pallas_call(
        _l0_pre_kernel,
        grid=(b, n_i + 1),
        in_specs=[
            pl.BlockSpec((None, TM, d), lambda bi, i: (bi, jnp.minimum(i, n_i - 1), 0)),
            pl.BlockSpec((None, 6, d), lambda bi, i: (bi, 0, 0)),
            _const_spec((1, d)),
            _const_spec((d, inw)),
            _const_spec((1, LANES)),
            _const_spec((1, LANES)),
            tab, tab, tab,
            _const_spec((LANES, LANES)),
            _const_spec((N_SG_GROUPS, CHUNK, CHUNK)),
            _const_spec((N_SG_GROUPS, CHUNK, LANES)),
        ],
        out_specs=[
            pl.BlockSpec((None, npair, LANES, TM), lambda bi, i: (bi, 0, 0, prev(i))),
            pl.BlockSpec((None, TM, kvw), row),
            pl.BlockSpec((None, kvw, TM), lambda bi, i: (bi, 0, prev(i))),
            pl.BlockSpec((None, TM, sgw), row),
        ],
        out_shape=[
            jax.ShapeDtypeStruct((b, npair, LANES, s), BF16),
            jax.ShapeDtypeStruct((b, s, kvw), BF16),
            jax.ShapeDtypeStruct((b, kvw, s), BF16),
            jax.ShapeDtypeStruct((b, s, sgw), BF16),
        ],
        scratch_shapes=[pltpu.VMEM((TM, inw), F32), pltpu.VMEM((TM, inw), F32)],
        compiler_params=_params(2),
        name="l0_pre",
    )(x, mod, g_mix, w_in, qg2, kg2, cos, sup, sdn, bd, w_sp, bsp)


def _ctx_kv_kernel(x_ref, mod_ref, g_ref, w_ref, kg_ref, bd_ref, k_ref, v_ref):
    kvw = N_KV_HEADS * HEAD_DIM
    xm = _rms_mod(x_ref[...], g_ref[...], mod_ref[1:2, :], mod_ref[0:1, :]).astype(BF16)
    p = _dot(xm, w_ref[...])
    k_ref[...] = _head_rms(p[:, 0:kvw], bd_ref[...], kg_ref[...]).astype(BF16)
    v_ref[...] = p[:, kvw:2 * kvw].T.astype(BF16)


def _ctx_kv(ctx, cmod, g_mix, w_kv, kg2, bd):
    b, m, d = ctx.shape
    kvw = N_KV_HEADS * HEAD_DIM
    row = lambda bi: (bi, 0, 0)
    return pl.pallas_call(
        _ctx_kv_kernel,
        grid=(b,),
        in_specs=[
            pl.BlockSpec((None, m, d), row),
            _const_spec((6, d)),
            _const_spec((1, d)),
            _const_spec((d, 2 * kvw)),
            _const_spec((1, LANES)),
            _const_spec((LANES, LANES)),
        ],
        out_specs=[pl.BlockSpec((None, m, kvw), row), pl.BlockSpec((None, kvw, m), row)],
        out_shape=[jax.ShapeDtypeStruct((b, m, kvw), BF16), jax.ShapeDtypeStruct((b, kvw, m), BF16)],
        compiler_params=_params(1),
        name="ctx_kv",
    )(ctx, cmod, g_mix, w_kv, kg2, bd)


NEG_INIT = -1e30
VX_ROWS = HEAD_DIM + 16


def _attn_kernel(q_ref, k_ref, kc_ref, v_ref, vc_ref, o_ref, ka_scr, kb_scr, vx_scr, s_scr, mx_scr, m_scr, acc_scr):
    h = pl.program_id(1)
    i = pl.program_id(2)
    tq = q_ref.shape[2]
    n_ctx = kc_ref.shape[0]
    n_keys = ka_scr.shape[0]
    chunks = [(r0, min(TK, n_ctx - r0)) for r0 in range(0, n_ctx, TK)]
    chunks += [(r0, TK) for r0 in range(n_ctx, n_keys, TK)]

    @pl.when(i == 0)
    def _fill():
        hrow = pl.ds(pl.multiple_of(h * HEAD_DIM, HEAD_DIM), HEAD_DIM)

        def put_k(kk, r0):
            lo = lax.broadcasted_iota(jnp.int32, kk.shape, 1) < HEAD_DIM
            kk = kk.astype(F32)
            ksw = pltpu.roll(kk, HEAD_DIM, 1)
            k_lo = jnp.where(h == 0, kk, ksw)
            k_hi = jnp.where(h == 0, ksw, kk)
            ka_scr[r0:r0 + kk.shape[0], :] = jnp.where(lo, k_lo, 0.0).astype(BF16)
            kb_scr[r0:r0 + kk.shape[0], :] = jnp.where(lo, 0.0, k_hi).astype(BF16)

        for r0, nk in chunks:
            if r0 < n_ctx:
                put_k(kc_ref[r0:r0 + nk, :], r0)
                vx_scr[0:HEAD_DIM, r0:r0 + nk] = vc_ref[hrow, r0:r0 + nk]
            else:
                put_k(k_ref[r0 - n_ctx:r0 - n_ctx + nk, :], r0)
                vx_scr[0:HEAD_DIM, r0:r0 + nk] = v_ref[hrow, r0 - n_ctx:r0 - n_ctx + nk]
            ones_row = lax.broadcasted_iota(jnp.int32, (VX_ROWS - HEAD_DIM, nk), 0) == 0
            vx_scr[HEAD_DIM:, r0:r0 + nk] = jnp.where(ones_row, 1.0, 0.0).astype(BF16)

    m_scr[...] = jnp.full(m_scr.shape, NEG_INIT, F32)
    acc_scr[...] = jnp.zeros(acc_scr.shape, F32)
    heads = range(m_scr.shape[0])

    def scores(c, g):
        r0, nk = chunks[c]
        k_scr = kb_scr if g % 2 else ka_scr
        s = _dot(k_scr[r0:r0 + nk, :], q_ref[g // 2])
        s_scr[c % 2, g, 0:nk, :] = s
        mx_scr[c % 2, g] = jnp.max(s, axis=0, keepdims=True)

    def softmax_pv(c, g):
        r0, nk = chunks[c]
        m_old = m_scr[g]
        m_new = jnp.maximum(m_old, mx_scr[c % 2, g])
        p = jnp.exp2(s_scr[c % 2, g, 0:nk, :] - m_new).astype(BF16)
        acc_scr[g] = jnp.exp2(m_old - m_new) * acc_scr[g] + _dot(vx_scr[:, r0:r0 + nk], p)
        m_scr[g] = m_new

    for g in heads:
        scores(0, g)
    for c in range(len(chunks)):
        for g in heads:
            if c + 1 < len(chunks):
                scores(c + 1, g)
            softmax_pv(c, g)

    for pair in range(len(heads) // 2):
        a = acc_scr[2 * pair]
        b = acc_scr[2 * pair + 1]
        st = jnp.concatenate([a[0:HEAD_DIM] / a[HEAD_DIM:HEAD_DIM + 1], b[0:HEAD_DIM] / b[HEAD_DIM:HEAD_DIM + 1]],
                             axis=0)
        o_ref[:, pair * LANES:(pair + 1) * LANES] = st.T.astype(BF16)


def _attention(q, k, kc, v, vc):
    b, npair, _, s = q.shape
    m = kc.shape[1]
    kvw = k.shape[2]
    pairs_per_kv = npair // N_KV_HEADS
    gw = pairs_per_kv * LANES
    assert s % TK == 0 and m % LANES == 0, (m, s, TK)
    full = lambda bi, h, i: (bi, 0, 0)
    return pl.pallas_call(
        _attn_kernel,
        grid=(b, N_KV_HEADS, s // TQ),
        in_specs=[
            pl.BlockSpec((None, pairs_per_kv, LANES, TQ), lambda bi, h, i: (bi, h, 0, i)),
            pl.BlockSpec((None, s, kvw), full),
            pl.BlockSpec((None, m, kvw), full),
            pl.BlockSpec((None, kvw, s), full),
            pl.BlockSpec((None, kvw, m), full),
        ],
        out_specs=pl.BlockSpec((None, TQ, gw), lambda bi, h, i: (bi, i, h)),
        out_shape=jax.ShapeDtypeStruct((b, s, N_KV_HEADS * gw), BF16),
        scratch_shapes=[
            pltpu.VMEM((m + s, LANES), BF16),
            pltpu.VMEM((m + s, LANES), BF16),
            pltpu.VMEM((VX_ROWS, m + s), BF16),
            pltpu.VMEM((2, 2 * pairs_per_kv, TK, TQ), F32),
            pltpu.VMEM((2, 2 * pairs_per_kv, 1, TQ), F32),
            pltpu.VMEM((2 * pairs_per_kv, 1, TQ), F32),
            pltpu.VMEM((2 * pairs_per_kv, VX_ROWS, TQ), F32),
        ],
        compiler_params=_params(3),
        name="attn",
    )(q, k, kc, v, vc)


def _ffn_residual(h, mod_ref, g_ref, wi_ref, wo_ref, xm_scr, acc_scr):
    dff = wo_ref.shape[0]
    xm_scr[...] = _rms_mod(h, g_ref[...], mod_ref[4:5, :], mod_ref[3:4, :]).astype(BF16)
    acc_scr[...] = jnp.zeros(acc_scr.shape, F32)

    for c0 in range(0, dff, FFN_CHUNK):
        xm = xm_scr[...]
        g = _dot(xm, wi_ref[:, c0:c0 + FFN_CHUNK])
        u = _dot(xm, wi_ref[:, dff + c0:dff + c0 + FFN_CHUNK])
        a = (g * _sigmoid(g) * u).astype(BF16)
        acc_scr[...] += _dot(a, wo_ref[c0:c0 + FFN_CHUNK, :])
    return h + mod_ref[5:6, :] * acc_scr[...]


def _layer_spec(w, layer):
    return pl.BlockSpec((None,) + w.shape[1:], lambda *_: (layer, 0, 0), pipeline_mode=pl.Buffered(1))


def _ffn_specs(d, wi, wd, layer):
    return [_const_spec((1, d)), _layer_spec(wi, layer), _layer_spec(wd, layer)]


def _ffn_scratch(d):
    return [pltpu.VMEM((TM, d), BF16), pltpu.VMEM((TM, d), F32)]


def _l0_post_kernel(x_ref, a_ref, sg_ref, mod_ref, wo_ref, gf_ref, wi_ref, wd_ref, mod1_ref, g1_ref, w1_ref, b1_ref,
                    h_ref, u_ref, xm_scr, acc_scr):
    aw = a_ref.shape[1]
    d = x_ref.shape[1]
    y = _dot(a_ref[...], wo_ref[0:aw, :]) + _dot(sg_ref[...], wo_ref[aw:, :])
    h = x_ref[...] + mod_ref[2:3, :] * y
    h = _ffn_residual(h, mod_ref, gf_ref, wi_ref, wd_ref, xm_scr, acc_scr)
    h_ref[...] = h
    xm_scr[...] = _rms_mod(h, g1_ref[...], mod1_ref[1:2, :], mod1_ref[0:1, :]).astype(BF16)
    for c0 in range(0, d, FFN_CHUNK):
        xm = xm_scr[...]
        a = _dot(xm, w1_ref[:, c0:c0 + FFN_CHUNK]) + b1_ref[:, c0:c0 + FFN_CHUNK]
        gate = _dot(xm, w1_ref[:, d + c0:d + c0 + FFN_CHUNK]) + b1_ref[:, d + c0:d + c0 + FFN_CHUNK]
        u_ref[:, c0:c0 + FFN_CHUNK] = a * _sigmoid(gate)


def _l0_post(x, attn, sg, mod, w_out, g_ffn, wi, wd, mod1, g_mix1, w_pw1, b_pw1):
    b, s, d = x.shape
    row = lambda bi, i: (bi, i, 0)
    mods = pl.BlockSpec((None, 6, d), lambda bi, i: (bi, 0, 0))
    tile = pl.BlockSpec((None, TM, d), row)
    return pl.pallas_call(
        _l0_post_kernel,
        grid=(b, s // TM),
        in_specs=[
            tile,
            pl.BlockSpec((None, TM, attn.shape[2]), row),
            pl.BlockSpec((None, TM, sg.shape[2]), row),
            mods,
            _const_spec(w_out.shape),
        ] + _ffn_specs(d, wi, wd, 0) + [mods, _const_spec((1, d)), _const_spec(w_pw1.shape), _const_spec((1, 2 * d))],
        out_specs=[tile, tile],
        out_shape=[jax.ShapeDtypeStruct((b, s, d), F32), jax.ShapeDtypeStruct((b, s, d), F32)],
        scratch_shapes=_ffn_scratch(d),
        compiler_params=_params(2),
        name="l0_post",
    )(x, attn, sg, mod, w_out, g_ffn, wi, wd, mod1, g_mix1, w_pw1, b_pw1)


def _l1_post_kernel(h_ref, u_ref, up_ref, un_ref, mod_ref, wdw_ref, bdw_ref, lng_ref, lnb_ref, w2_ref, b2_ref,
                    gf_ref, wg_ref, wu_ref, wd_ref, gfin_ref, o_ref, e_scr, c_scr, xm_scr, acc_scr,
                    *, tiles_per_seq, n_tiles):
    t = pl.program_id(0)
    tm = h_ref.shape[0]
    nslab = e_scr.shape[0]
    nck = wg_ref.shape[0]

    @pl.when(t == 0)
    def _zero():
        c_scr[...] = jnp.zeros(c_scr.shape, F32)

    cv = jnp.concatenate([c_scr[sl] for sl in range(nslab)], axis=1)
    mu = jnp.mean(cv, axis=-1, keepdims=True)
    dv = cv - mu
    var = jnp.mean(dv * dv, axis=-1, keepdims=True)
    ln = dv * lax.rsqrt(var + EPS) * lng_ref[...] + lnb_ref[...]
    act = (ln * _sigmoid(ln)).astype(BF16)
    y = _dot(act, w2_ref[...]) + b2_ref[...]
    h = h_ref[...] + mod_ref[2:3, :] * y

    i = lax.rem(jnp.minimum(t, n_tiles - 1), tiles_per_seq)
    up = jnp.where(i > 0, up_ref[...], 0.0)
    un = jnp.where(i < tiles_per_seq - 1, un_ref[...], 0.0)
    for sl in range(nslab):
        cols = slice(sl * LANES, (sl + 1) * LANES)
        e_scr[sl, 0:HALO, :] = up[:, cols]
        e_scr[sl, HALO:HALO + tm, :] = u_ref[:, cols]
        e_scr[sl, HALO + tm:, :] = un[:, cols]
    rb = 64
    off = HALO - CONV_WIDTH // 2

    xm_scr[...] = _rms_mod(h, gf_ref[...], mod_ref[4:5, :], mod_ref[3:4, :]).astype(BF16)
    acc_scr[...] = jnp.zeros(acc_scr.shape, F32)

    def ffn_chunk(c):
        xm = xm_scr[...]
        g = _dot(xm, wg_ref[c])
        u = _dot(xm, wu_ref[c])
        a = (g * _sigmoid(g) * u).astype(BF16)
        acc_scr[...] += _dot(a, wd_ref[c])

    def chunk_and_slab(c, carry):
        ffn_chunk(c)
        for r0 in range(0, tm, rb):
            acc = jnp.broadcast_to(bdw_ref[c], (rb, LANES))
            for j in range(CONV_WIDTH):
                acc = acc + wdw_ref[c, j:j + 1, :] * e_scr[c, r0 + off + j:r0 + off + j + rb, :]
            c_scr[c, r0:r0 + rb, :] = acc
        return carry

    lax.fori_loop(0, nslab, chunk_and_slab, 0)
    for c in range(nslab, nck):
        ffn_chunk(c)
    h = h + mod_ref[5:6, :] * acc_scr[...]
    ms = jnp.mean(h * h, axis=-1, keepdims=True)
    o_ref[...] = h * lax.rsqrt(ms + EPS) * gfin_ref[...]


def _l1_post(h, u, mod, w_dw, b_dw, ln_g, ln_b, w_pw2, b_pw2, g_ffn, wg, wu, wd, g_final):
    b, s, d = h.shape
    tps = s // TM
    n_tiles = b * tps
    per = TM // HALO
    last = s // HALO - 1
    vec = _const_spec((1, d))
    assert d // LANES <= wg.shape[0], (d, wg.shape)

    def conv_tile(t):
        tc = jnp.minimum(t, n_tiles - 1)
        return tc // tps, lax.rem(tc, tps)

    def tail_tile(t):
        td = jnp.maximum(t - 1, 0)
        return td // tps, lax.rem(td, tps)

    def tail_row(t):
        bi, i = tail_tile(t)
        return bi, i, 0

    def conv_row(t):
        bi, i = conv_tile(t)
        return bi, i, 0

    def conv_prev(t):
        bi, i = conv_tile(t)
        return bi, jnp.maximum(i * per - 1, 0), 0

    def conv_next(t):
        bi, i = conv_tile(t)
        return bi, jnp.minimum((i + 1) * per, last), 0

    return pl.pallas_call(
        functools.partial(_l1_post_kernel, tiles_per_seq=tps, n_tiles=n_tiles),
        grid=(n_tiles + 1,),
        in_specs=[
            pl.BlockSpec((None, TM, d), tail_row),
            pl.BlockSpec((None, TM, d), conv_row),
            pl.BlockSpec((None, HALO, d), conv_prev),
            pl.BlockSpec((None, HALO, d), conv_next),
            pl.BlockSpec((None, 6, d), lambda t: (tail_tile(t)[0], 0, 0)),
            _const_spec(w_dw.shape), _const_spec(b_dw.shape), vec, vec,
            _const_spec(w_pw2.shape), vec,
            vec, _const_spec(wg.shape), _const_spec(wu.shape), _const_spec(wd.shape), vec,
        ],
        out_specs=pl.BlockSpec((None, TM, d), tail_row),
        out_shape=jax.ShapeDtypeStruct((b, s, d), F32),
        scratch_shapes=[pltpu.VMEM((d // LANES, TM + 2 * HALO, LANES), F32),
                        pltpu.VMEM((d // LANES, TM, LANES), F32)] + _ffn_scratch(d),
        compiler_params=_params(1),
        name="l1_post",
    )(h, u, u, u, mod, w_dw, b_dw, ln_g, ln_b, w_pw2, b_pw2, g_ffn, wg, wu, wd, g_final)


def _rope_tables(n):
    half = HEAD_DIM // 2
    pos = np.arange(n)
    row = (pos // GRID_W).astype(np.float32)
    col = (pos % GRID_W).astype(np.float32)
    inv = (ROPE_THETA ** (-np.arange(0, half, 2, dtype=np.float32) / half)).astype(np.float32)
    lane = np.arange(LANES)
    hd = lane % HEAD_DIM
    w = hd % half
    ang = np.where((hd < half)[None, :], row[:, None], col[:, None]) * inv[w % (half // 2)][None, :]
    first = (w < half // 2)[None, :]
    sin = np.sin(ang)
    tabs = (np.cos(ang), np.where(first, -sin, 0.0), np.where(first, 0.0, sin))
    return tuple(jnp.asarray(t, F32) for t in tabs)


def kernel(x, c, ctx, c_ctx, w_mod, b_mod, g_mix, g_ffn, w_ffn_in, w_ffn_out, w_in, q_gain, k_gain, w_sp, b_sp,
           w_out, w_pw1, b_pw1, w_dw, b_dw, ln_g, ln_b, w_pw2, b_pw2, g_final):
    b, s, d = x.shape
    aw = N_Q_HEADS * HEAD_DIM
    kvw = N_KV_HEADS * HEAD_DIM

    rows = -(-(b + 1) // 8) * 8
    cvec = jnp.concatenate([c, c_ctx[None, :], jnp.zeros((rows - b - 1, d), F32)], axis=0)
    mod = _adaln(cvec, w_mod, b_mod)
    mod0 = mod[0, :b].reshape(b, 6, d)
    cmod0 = mod[0, b].reshape(6, d)
    mod1 = mod[1, :b].reshape(b, 6, d)

    cos, sup, sdn = _rope_tables(s)
    lane = jnp.arange(LANES)
    bd = (lane[:, None] // HEAD_DIM == lane[None, :] // HEAD_DIM).astype(BF16)
    qg2 = jnp.tile(q_gain[0], LANES // HEAD_DIM)[None, :]
    kg2 = jnp.tile(k_gain[0], LANES // HEAD_DIM)[None, :]
    w_in_b = w_in[0].astype(BF16)
    bsp = jnp.broadcast_to(b_sp[0][:, :, None], (N_SG_GROUPS, CHUNK, LANES))

    q, k, v, sg = _l0_pre(x, mod0, g_mix[0:1], w_in_b, qg2, kg2, cos, sup, sdn, bd, w_sp[0].astype(BF16), bsp)
    kc, vc = _ctx_kv(ctx, cmod0, g_mix[0:1], w_in_b[:, aw:aw + 2 * kvw], kg2, bd)
    attn = _attention(q, k, kc, v, vc)

    assert w_ffn_out.shape[1] % FFN_CHUNK == 0 and d % FFN_CHUNK == 0, (w_ffn_out.shape, d)
    wi_b = w_ffn_in.astype(BF16)
    wd_b = w_ffn_out.astype(BF16)
    h, u = _l0_post(x, attn, sg, mod0, w_out[0].astype(BF16), g_ffn[0:1], wi_b, wd_b,
                    mod1, g_mix[1:2], w_pw1[0].astype(BF16), b_pw1[0:1])
    w_dw_s = w_dw[0].reshape(CONV_WIDTH, d // LANES, LANES).transpose(1, 0, 2)
    b_dw_s = b_dw[0].reshape(d // LANES, 1, LANES)
    dff = w_ffn_out.shape[1]
    nck = dff // FFN_CHUNK
    wg1 = wi_b[1, :, :dff].reshape(d, nck, FFN_CHUNK).transpose(1, 0, 2)
    wu1 = wi_b[1, :, dff:].reshape(d, nck, FFN_CHUNK).transpose(1, 0, 2)
    wd1 = wd_b[1].reshape(nck, FFN_CHUNK, d)
    return _l1_post(h, u, mod1, w_dw_s, b_dw_s, ln_g[0:1], ln_b[0:1], w_pw2[0].astype(BF16), b_pw2[0:1],
                    g_ffn[1:2], wg1, wu1, wd1, g_final[None, :])
```

```python
import functools

import jax
import jax.numpy as jnp
import numpy as np
from jax import lax
from jax.experimental import pallas as pl
from jax.experimental.pallas import tpu as pltpu

F32 = jnp.float32
BF16 = jnp.bfloat16

HEAD_DIM = 64
N_Q_HEADS = 8
N_KV_HEADS = 2
GRID_W = 64
CHUNK = 128
N_SG_GROUPS = 4
ROPE_THETA = 10000.0
CONV_WIDTH = 31
EPS = 1e-6
Q_SCALE = HEAD_DIM ** -0.5 * 1.4426950408889634

LANES = 128
HALO = 16
FFN_CHUNK = 256
VMEM_LIMIT = 56 * 1024 * 1024

TM = 512
TQ = 512
TK = 512


def _const_spec(shape):
    nd = len(shape)
    return pl.BlockSpec(shape, lambda *_: (0,) * nd, pipeline_mode=pl.Buffered(1))


def _params(n_axes, flags=None):
    return pltpu.CompilerParams(dimension_semantics=("arbitrary",) * n_axes,
                                vmem_limit_bytes=VMEM_LIMIT, flags=flags)


def _dot(a, b):
    return jnp.dot(a, b, preferred_element_type=F32)


def _sigmoid(x):
    return 1.0 / (1.0 + jnp.exp(-x))


def _gelu_tanh(x):
    c = 0.7978845608028654
    return x * (0.5 * (1.0 + jnp.tanh(c * (x + 0.044715 * (x * x * x)))))


def _rms_mod(x, g, sc, sh):
    ms = jnp.mean(x * x, axis=-1, keepdims=True)
    return (x * lax.rsqrt(ms + EPS) * g) * (1.0 + sc) + sh


def _head_rms(z, bd, gain):
    z2 = z * z
    hi = z2.astype(BF16)
    lo = (z2 - hi.astype(F32)).astype(BF16)
    ssq = _dot(hi, bd) + _dot(lo, bd)
    return z * lax.rsqrt(ssq * (1.0 / HEAD_DIM) + EPS) * gain


def _rope(z, cos, s_up, s_dn):
    return z * cos + pltpu.roll(z, LANES - 16, 1) * s_up + pltpu.roll(z, 16, 1) * s_dn


def _adaln_kernel(c_ref, w_ref, b_ref, o_ref):
    c = c_ref[...]
    s = (c * _sigmoid(c)).astype(BF16)
    o_ref[...] = _dot(s, w_ref[...].astype(BF16)) + b_ref[...]


def _adaln(cvec, w_mod, b_mod):
    depth, d, n = w_mod.shape
    rows = cvec.shape[0]
    tn = 1536
    return pl.pallas_call(
        _adaln_kernel,
        grid=(depth, n // tn),
        in_specs=[
            pl.BlockSpec((rows, d), lambda l, j: (0, 0)),
            pl.BlockSpec((None, d, tn), lambda l, j: (l, 0, j)),
            pl.BlockSpec((None, 1, tn), lambda l, j: (l, 0, j)),
        ],
        out_specs=pl.BlockSpec((None, rows, tn), lambda l, j: (l, 0, j)),
        out_shape=jax.ShapeDtypeStruct((depth, rows, n), F32),
        compiler_params=_params(2),
        name="adaln",
    )(cvec, w_mod, b_mod.reshape(depth, 1, n))


def _l0_pre_kernel(x_ref, mod_ref, g_ref, w_ref, qg_ref, kg_ref, cos_ref, sup_ref, sdn_ref, bd_ref,
                   wsp_ref, bsp_ref, q_ref, k_ref, v_ref, sg_ref, pa_scr, pb_scr):
    i = pl.program_id(1)
    aw = N_Q_HEADS * HEAD_DIM
    kvw = N_KV_HEADS * HEAD_DIM
    sgw = N_SG_GROUPS * LANES
    o_u = aw + 2 * kvw
    o_v = o_u + sgw
    tm = x_ref.shape[0]
    nchunk = tm // CHUNK

    @pl.when(i == 0)
    def _zero():
        pb_scr[...] = jnp.zeros(pb_scr.shape, F32)

    def step(p_new, p_old):
        xm = _rms_mod(x_ref[...], g_ref[...], mod_ref[1:2, :], mod_ref[0:1, :]).astype(BF16)
        p_new[...] = _dot(xm, w_ref[...])
        bd = bd_ref[...]
        cos, sup, sdn = cos_ref[...], sup_ref[...], sdn_ref[...]
        for j in range(aw // LANES):
            qn = _head_rms(p_old[:, j * LANES:(j + 1) * LANES], bd, qg_ref[...])
            q_ref[j] = (_rope(qn, cos, sup, sdn) * Q_SCALE).T.astype(BF16)
        kn = _head_rms(p_old[:, aw:aw + kvw], bd, kg_ref[...])
        k_ref[...] = _rope(kn, cos, sup, sdn).astype(BF16)
        v_ref[...] = p_old[:, aw + kvw:aw + 2 * kvw].T.astype(BF16)
        for g in range(N_SG_GROUPS):
            gv = _gelu_tanh(p_old[:, o_v + g * LANES:o_v + (g + 1) * LANES])
            mu = jnp.mean(gv, axis=-1, keepdims=True)
            dv = gv - mu
            var = jnp.mean(dv * dv, axis=-1, keepdims=True)
            vn = (dv * lax.rsqrt(var + EPS)).astype(BF16)
            rhs = jnp.concatenate([vn[c * CHUNK:(c + 1) * CHUNK, :] for c in range(nchunk)], axis=1)
            mixed = _dot(wsp_ref[g], rhs)
            gu = _gelu_tanh(p_old[:, o_u + g * LANES:o_u + (g + 1) * LANES])
            for c in range(nchunk):
                blk = mixed[:, c * CHUNK:(c + 1) * CHUNK] + bsp_ref[g]
                sg_ref[c * CHUNK:(c + 1) * CHUNK, g * LANES:(g + 1) * LANES] = (
                    gu[c * CHUNK:(c + 1) * CHUNK, :] * blk).astype(BF16)

    @pl.when(lax.rem(i, 2) == 0)
    def _even():
        step(pa_scr, pb_scr)

    @pl.when(lax.rem(i, 2) == 1)
    def _odd():
        step(pb_scr, pa_scr)


def _l0_pre(x, mod, g_mix, w_in, qg2, kg2, cos, sup, sdn, bd, w_sp, bsp):
    b, s, d = x.shape
    inw = w_in.shape[1]
    npair = N_Q_HEADS * HEAD_DIM // LANES
    kvw = N_KV_HEADS * HEAD_DIM
    sgw = N_SG_GROUPS * LANES
    n_i = s // TM
    prev = lambda i: jnp.maximum(i - 1, 0)
    row = lambda bi, i: (bi, prev(i), 0)
    tab = pl.BlockSpec((TM, LANES), lambda bi, i: (prev(i), 0))
    return pl.pallas_call(
        _l0_pre_kernel,
        grid=(b, n_i + 1),
        in_specs=[
            pl.BlockSpec((None, TM, d), lambda bi, i: (bi, jnp.minimum(i, n_i - 1), 0)),
            pl.BlockSpec((None, 6, d), lambda bi, i: (bi, 0, 0)),
            _const_spec((1, d)),
            _const_spec((d, inw)),
            _const_spec((1, LANES)),
            _const_spec((1, LANES)),
            tab, tab, tab,
            _const_spec((LANES, LANES)),
            _const_spec((N_SG_GROUPS, CHUNK, CHUNK)),
            _const_spec((N_SG_GROUPS, CHUNK, LANES)),
        ],
        out_specs=[
            pl.BlockSpec((None, npair, LANES, TM), lambda bi, i: (bi, 0, 0, prev(i))),
            pl.BlockSpec((None, TM, kvw), row),
            pl.BlockSpec((None, kvw, TM), lambda bi, i: (bi, 0, prev(i))),
            pl.BlockSpec((None, TM, sgw), row),
        ],
        out_shape=[
            jax.ShapeDtypeStruct((b, npair, LANES, s), BF16),
            jax.ShapeDtypeStruct((b, s, kvw), BF16),
            jax.ShapeDtypeStruct((b, kvw, s), BF16),
            jax.ShapeDtypeStruct((b, s, sgw), BF16),
        ],
        scratch_shapes=[pltpu.VMEM((TM, inw), F32), pltpu.VMEM((TM, inw), F32)],
        compiler_params=_params(2),
        name="l0_pre",
    )(x, mod, g_mix, w_in, qg2, kg2, cos, sup, sdn, bd, w_sp, bsp)


def _ctx_kv_kernel(x_ref, mod_ref, g_ref, w_ref, kg_ref, bd_ref, k_ref, v_ref):
    kvw = N_KV_HEADS * HEAD_DIM
    xm = _rms_mod(x_ref[...], g_ref[...], mod_ref[1:2, :], mod_ref[0:1, :]).astype(BF16)
    p = _dot(xm, w_ref[...])
    k_ref[...] = _head_rms(p[:, 0:kvw], bd_ref[...], kg_ref[...]).astype(BF16)
    v_ref[...] = p[:, kvw:2 * kvw].T.astype(BF16)


def _ctx_kv(ctx, cmod, g_mix, w_kv, kg2, bd):
    b, m, d = ctx.shape
    kvw = N_KV_HEADS * HEAD_DIM
    row = lambda bi: (bi, 0, 0)
    return pl.pallas_call(
        _ctx_kv_kernel,
        grid=(b,),
        in_specs=[
            pl.BlockSpec((None, m, d), row),
            _const_spec((6, d)),
            _const_spec((1, d)),
            _const_spec((d, 2 * kvw)),
            _const_spec((1, LANES)),
            _const_spec((LANES, LANES)),
        ],
        out_specs=[pl.BlockSpec((None, m, kvw), row), pl.BlockSpec((None, kvw, m), row)],
        out_shape=[jax.ShapeDtypeStruct((b, m, kvw), BF16), jax.ShapeDtypeStruct((b, kvw, m), BF16)],
        compiler_params=_params(1),
        name="ctx_kv",
    )(ctx, cmod, g_mix, w_kv, kg2, bd)


NEG_INIT = -1e30
VX_ROWS = HEAD_DIM + 16
P_ROWS = 32


def _attn_kernel(q_ref, k_ref, kc_ref, v_ref, vc_ref, o_ref, ka_scr, kb_scr, vx_scr, s_scr, p_scr, mx_scr, m_scr,
                 acc_scr):
    h = pl.program_id(1)
    i = pl.program_id(2)
    tq = q_ref.shape[2]
    n_ctx = kc_ref.shape[0]
    n_keys = ka_scr.shape[0]
    chunks = [(r0, min(TK, n_ctx - r0)) for r0 in range(0, n_ctx, TK)]
    chunks += [(r0, TK) for r0 in range(n_ctx, n_keys, TK)]

    @pl.when(i == 0)
    def _fill():
        hrow = pl.ds(pl.multiple_of(h * HEAD_DIM, HEAD_DIM), HEAD_DIM)

        def put_k(kk, r0):
            lo = lax.broadcasted_iota(jnp.int32, kk.shape, 1) < HEAD_DIM
            kk = kk.astype(F32)
            ksw = pltpu.roll(kk, HEAD_DIM, 1)
            k_lo = jnp.where(h == 0, kk, ksw)
            k_hi = jnp.where(h == 0, ksw, kk)
            ka_scr[r0:r0 + kk.shape[0], :] = jnp.where(lo, k_lo, 0.0).astype(BF16)
            kb_scr[r0:r0 + kk.shape[0], :] = jnp.where(lo, 0.0, k_hi).astype(BF16)

        for r0, nk in chunks:
            if r0 < n_ctx:
                put_k(kc_ref[r0:r0 + nk, :], r0)
                vx_scr[0:HEAD_DIM, r0:r0 + nk] = vc_ref[hrow, r0:r0 + nk]
            else:
                put_k(k_ref[r0 - n_ctx:r0 - n_ctx + nk, :], r0)
                vx_scr[0:HEAD_DIM, r0:r0 + nk] = v_ref[hrow, r0 - n_ctx:r0 - n_ctx + nk]
            ones_row = lax.broadcasted_iota(jnp.int32, (VX_ROWS - HEAD_DIM, nk), 0) == 0
            vx_scr[HEAD_DIM:, r0:r0 + nk] = jnp.where(ones_row, 1.0, 0.0).astype(BF16)

    m_scr[...] = jnp.full(m_scr.shape, NEG_INIT, F32)
    acc_scr[...] = jnp.zeros(acc_scr.shape, F32)
    heads = range(m_scr.shape[0])
    dyn0 = jnp.minimum(i, 0)

    def scores(c, g):
        r0, nk = chunks[c]
        k_scr = kb_scr if g % 2 else ka_scr
        s = _dot(k_scr[r0:r0 + nk, :], q_ref[g // 2])
        s_scr[c % 2 + dyn0, g, 0:nk, :] = s
        mx_scr[c % 2, g] = jnp.max(s, axis=0, keepdims=True)

    def softmax_pv(c, g):
        r0, nk = chunks[c]
        m_old = m_scr[g]
        m_new = jnp.maximum(m_old, mx_scr[c % 2, g])
        for r in range(0, nk, P_ROWS):
            p_scr[g % 2 + dyn0, r:r + P_ROWS, :] = jnp.exp2(
                s_scr[c % 2 + dyn0, g, r:r + P_ROWS, :] - m_new).astype(BF16)
        pv = _dot(vx_scr[:, r0:r0 + nk], p_scr[g % 2 + dyn0, 0:nk, :])
        acc_scr[g] = jnp.exp2(m_old - m_new) * acc_scr[g] + pv
        m_scr[g] = m_new

    for g in heads:
        scores(0, g)
    for c in range(len(chunks)):
        for g in heads:
            if c + 1 < len(chunks):
                scores(c + 1, g)
            softmax_pv(c, g)

    for pair in range(len(heads) // 2):
        a = acc_scr[2 * pair]
        b = acc_scr[2 * pair + 1]
        st = jnp.concatenate([a[0:HEAD_DIM] / a[HEAD_DIM:HEAD_DIM + 1], b[0:HEAD_DIM] / b[HEAD_DIM:HEAD_DIM + 1]],
                             axis=0)
        o_ref[:, pair * LANES:(pair + 1) * LANES] = st.T.astype(BF16)


def _attention(q, k, kc, v, vc):
    b, npair, _, s = q.shape
    m = kc.shape[1]
    kvw = k.shape[2]
    pairs_per_kv = npair // N_KV_HEADS
    gw = pairs_per_kv * LANES
    assert s % TK == 0 and m % LANES == 0, (m, s, TK)
    full = lambda bi, h, i: (bi, 0, 0)
    return pl.pallas_call(
        _attn_kernel,
        grid=(b, N_KV_HEADS, s // TQ),
        in_specs=[
            pl.BlockSpec((None, pairs_per_kv, LANES, TQ), lambda bi, h, i: (bi, h, 0, i)),
            pl.BlockSpec((None, s, kvw), full),
            pl.BlockSpec((None, m, kvw), full),
            pl.BlockSpec((None, kvw, s), full),
            pl.BlockSpec((None, kvw, m), full),
        ],
        out_specs=pl.BlockSpec((None, TQ, gw), lambda bi, h, i: (bi, i, h)),
        out_shape=jax.ShapeDtypeStruct((b, s, N_KV_HEADS * gw), BF16),
        scratch_shapes=[
            pltpu.VMEM((m + s, LANES), BF16),
            pltpu.VMEM((m + s, LANES), BF16),
            pltpu.VMEM((VX_ROWS, m + s), BF16),
            pltpu.VMEM((2, 2 * pairs_per_kv, TK, TQ), F32),
            pltpu.VMEM((2, TK, TQ), BF16),
            pltpu.VMEM((2, 2 * pairs_per_kv, 1, TQ), F32),
            pltpu.VMEM((2 * pairs_per_kv, 1, TQ), F32),
            pltpu.VMEM((2 * pairs_per_kv, VX_ROWS, TQ), F32),
        ],
        compiler_params=_params(3),
        name="attn",
    )(q, k, kc, v, vc)


def _ffn_residual(h, mod_ref, g_ref, wi_ref, wo_ref, xm_scr, acc_scr):
    dff = wo_ref.shape[0]
    xm_scr[...] = _rms_mod(h, g_ref[...], mod_ref[4:5, :], mod_ref[3:4, :]).astype(BF16)
    acc_scr[...] = jnp.zeros(acc_scr.shape, F32)

    for c0 in range(0, dff, FFN_CHUNK):
        xm = xm_scr[...]
        g = _dot(xm, wi_ref[:, c0:c0 + FFN_CHUNK])
        u = _dot(xm, wi_ref[:, dff + c0:dff + c0 + FFN_CHUNK])
        a = (g * _sigmoid(g) * u).astype(BF16)
        acc_scr[...] += _dot(a, wo_ref[c0:c0 + FFN_CHUNK, :])
    return h + mod_ref[5:6, :] * acc_scr[...]


def _layer_spec(w, layer):
    return pl.BlockSpec((None,) + w.shape[1:], lambda *_: (layer, 0, 0), pipeline_mode=pl.Buffered(1))


def _ffn_specs(d, wi, wd, layer):
    return [_const_spec((1, d)), _layer_spec(wi, layer), _layer_spec(wd, layer)]


def _ffn_scratch(d):
    return [pltpu.VMEM((TM, d), BF16), pltpu.VMEM((TM, d), F32)]


def _l0_post_kernel(x_ref, a_ref, sg_ref, mod_ref, wo_ref, gf_ref, wi_ref, wd_ref, mod1_ref, g1_ref, w1_ref, b1_ref,
                    h_ref, u_ref, xm_scr, acc_scr):
    aw = a_ref.shape[1]
    d = x_ref.shape[1]
    y = _dot(a_ref[...], wo_ref[0:aw, :]) + _dot(sg_ref[...], wo_ref[aw:, :])
    h = x_ref[...] + mod_ref[2:3, :] * y
    h = _ffn_residual(h, mod_ref, gf_ref, wi_ref, wd_ref, xm_scr, acc_scr)
    h_ref[...] = h
    xm_scr[...] = _rms_mod(h, g1_ref[...], mod1_ref[1:2, :], mod1_ref[0:1, :]).astype(BF16)
    for c0 in range(0, d, FFN_CHUNK):
        xm = xm_scr[...]
        a = _dot(xm, w1_ref[:, c0:c0 + FFN_CHUNK]) + b1_ref[:, c0:c0 + FFN_CHUNK]
        gate = _dot(xm, w1_ref[:, d + c0:d + c0 + FFN_CHUNK]) + b1_ref[:, d + c0:d + c0 + FFN_CHUNK]
        u_ref[:, c0:c0 + FFN_CHUNK] = a * _sigmoid(gate)


def _l0_post(x, attn, sg, mod, w_out, g_ffn, wi, wd, mod1, g_mix1, w_pw1, b_pw1):
    b, s, d = x.shape
    row = lambda bi, i: (bi, i, 0)
    mods = pl.BlockSpec((None, 6, d), lambda bi, i: (bi, 0, 0))
    tile = pl.BlockSpec((None, TM, d), row)
    return pl.pallas_call(
        _l0_post_kernel,
        grid=(b, s // TM),
        in_specs=[
            tile,
            pl.BlockSpec((None, TM, attn.shape[2]), row),
            pl.BlockSpec((None, TM, sg.shape[2]), row),
            mods,
            _const_spec(w_out.shape),
        ] + _ffn_specs(d, wi, wd, 0) + [mods, _const_spec((1, d)), _const_spec(w_pw1.shape), _const_spec((1, 2 * d))],
        out_specs=[tile, tile],
        out_shape=[jax.ShapeDtypeStruct((b, s, d), F32), jax.ShapeDtypeStruct((b, s, d), F32)],
        scratch_shapes=_ffn_scratch(d),
        compiler_params=_params(2),
        name="l0_post",
    )(x, attn, sg, mod, w_out, g_ffn, wi, wd, mod1, g_mix1, w_pw1, b_pw1)


def _l1_post_kernel(h_ref, u_ref, up_ref, un_ref, mod_ref, wdw_ref, bdw_ref, lng_ref, lnb_ref, w2_ref, b2_ref,
                    gf_ref, wi_ref, wd_ref, gfin_ref, o_ref, e_scr, c_scr, xm_scr, acc_scr):
    i = pl.program_id(1)
    n_i = pl.num_programs(1)
    tm = h_ref.shape[0]
    nslab = e_scr.shape[0]
    up = jnp.where(i > 0, up_ref[...], 0.0)
    un = jnp.where(i < n_i - 1, un_ref[...], 0.0)
    for sl in range(nslab):
        cols = slice(sl * LANES, (sl + 1) * LANES)
        e_scr[sl, 0:HALO, :] = up[:, cols]
        e_scr[sl, HALO:HALO + tm, :] = u_ref[:, cols]
        e_scr[sl, HALO + tm:, :] = un[:, cols]
    rb = 64
    off = HALO - CONV_WIDTH // 2

    def conv_slab(sl, carry):
        for r0 in range(0, tm, rb):
            acc = jnp.broadcast_to(bdw_ref[sl], (rb, LANES))
            for j in range(CONV_WIDTH):
                acc = acc + wdw_ref[sl, j:j + 1, :] * e_scr[sl, r0 + off + j:r0 + off + j + rb, :]
            c_scr[sl, r0:r0 + rb, :] = acc
        return carry

    lax.fori_loop(0, nslab, conv_slab, 0)
    cv = jnp.concatenate([c_scr[sl] for sl in range(nslab)], axis=1)
    mu = jnp.mean(cv, axis=-1, keepdims=True)
    dv = cv - mu
    var = jnp.mean(dv * dv, axis=-1, keepdims=True)
    ln = dv * lax.rsqrt(var + EPS) * lng_ref[...] + lnb_ref[...]
    act = (ln * _sigmoid(ln)).astype(BF16)
    y = _dot(act, w2_ref[...]) + b2_ref[...]
    h = h_ref[...] + mod_ref[2:3, :] * y
    h = _ffn_residual(h, mod_ref, gf_ref, wi_ref, wd_ref, xm_scr, acc_scr)
    ms = jnp.mean(h * h, axis=-1, keepdims=True)
    o_ref[...] = h * lax.rsqrt(ms + EPS) * gfin_ref[...]


def _l1_post(h, u, mod, w_dw, b_dw, ln_g, ln_b, w_pw2, b_pw2, g_ffn, wi, wd, g_final):
    b, s, d = h.shape
    row = lambda bi, i: (bi, i, 0)
    per = TM // HALO
    last = s // HALO - 1
    vec = _const_spec((1, d))
    return pl.pallas_call(
        _l1_post_kernel,
        grid=(b, s // TM),
        in_specs=[
            pl.BlockSpec((None, TM, d), row),
            pl.BlockSpec((None, TM, d), row),
            pl.BlockSpec((None, HALO, d), lambda bi, i: (bi, jnp.maximum(i * per - 1, 0), 0)),
            pl.BlockSpec((None, HALO, d), lambda bi, i: (bi, jnp.minimum((i + 1) * per, last), 0)),
            pl.BlockSpec((None, 6, d), lambda bi, i: (bi, 0, 0)),
            _const_spec(w_dw.shape), _const_spec(b_dw.shape), vec, vec,
            _const_spec(w_pw2.shape), vec,
        ] + _ffn_specs(d, wi, wd, 1) + [vec],
        out_specs=pl.BlockSpec((None, TM, d), row),
        out_shape=jax.ShapeDtypeStruct((b, s, d), F32),
        scratch_shapes=[pltpu.VMEM((d // LANES, TM + 2 * HALO, LANES), F32),
                        pltpu.VMEM((d // LANES, TM, LANES), F32)] + _ffn_scratch(d),
        compiler_params=_params(2),
        name="l1_post",
    )(h, u, u, u, mod, w_dw, b_dw, ln_g, ln_b, w_pw2, b_pw2, g_ffn, wi, wd, g_final)


def _rope_tables(n):
    half = HEAD_DIM // 2
    pos = np.arange(n)
    row = (pos // GRID_W).astype(np.float32)
    col = (pos % GRID_W).astype(np.float32)
    inv = (ROPE_THETA ** (-np.arange(0, half, 2, dtype=np.float32) / half)).astype(np.float32)
    lane = np.arange(LANES)
    hd = lane % HEAD_DIM
    w = hd % half
    ang = np.where((hd < half)[None, :], row[:, None], col[:, None]) * inv[w % (half // 2)][None, :]
    first = (w < half // 2)[None, :]
    sin = np.sin(ang)
    tabs = (np.cos(ang), np.where(first, -sin, 0.0), np.where(first, 0.0, sin))
    return tuple(jnp.asarray(t, F32) for t in tabs)


def kernel(x, c, ctx, c_ctx, w_mod, b_mod, g_mix, g_ffn, w_ffn_in, w_ffn_out, w_in, q_gain, k_gain, w_sp, b_sp,
           w_out, w_pw1, b_pw1, w_dw, b_dw, ln_g, ln_b, w_pw2, b_pw2, g_final):
    b, s, d = x.shape
    aw = N_Q_HEADS * HEAD_DIM
    kvw = N_KV_HEADS * HEAD_DIM

    rows = -(-(b + 1) // 8) * 8
    cvec = jnp.concatenate([c, c_ctx[None, :], jnp.zeros((rows - b - 1, d), F32)], axis=0)
    mod = _adaln(cvec, w_mod, b_mod)
    mod0 = mod[0, :b].reshape(b, 6, d)
    cmod0 = mod[0, b].reshape(6, d)
    mod1 = mod[1, :b].reshape(b, 6, d)

    cos, sup, sdn = _rope_tables(s)
    lane = jnp.arange(LANES)
    bd = (lane[:, None] // HEAD_DIM == lane[None, :] // HEAD_DIM).astype(BF16)
    qg2 = jnp.tile(q_gain[0], LANES // HEAD_DIM)[None, :]
    kg2 = jnp.tile(k_gain[0], LANES // HEAD_DIM)[None, :]
    w_in_b = w_in[0].astype(BF16)
    bsp = jnp.broadcast_to(b_sp[0][:, :, None], (N_SG_GROUPS, CHUNK, LANES))

    q, k, v, sg = _l0_pre(x, mod0, g_mix[0:1], w_in_b, qg2, kg2, cos, sup, sdn, bd, w_sp[0].astype(BF16), bsp)
    kc, vc = _ctx_kv(ctx, cmod0, g_mix[0:1], w_in_b[:, aw:aw + 2 * kvw], kg2, bd)
    attn = _attention(q, k, kc, v, vc)

    assert w_ffn_out.shape[1] % FFN_CHUNK == 0 and d % FFN_CHUNK == 0, (w_ffn_out.shape, d)
    wi_b = w_ffn_in.astype(BF16)
    wd_b = w_ffn_out.astype(BF16)
    h, u = _l0_post(x, attn, sg, mod0, w_out[0].astype(BF16), g_ffn[0:1], wi_b, wd_b,
                    mod1, g_mix[1:2], w_pw1[0].astype(BF16), b_pw1[0:1])
    w_dw_s = w_dw[0].reshape(CONV_WIDTH, d // LANES, LANES).transpose(1, 0, 2)
    b_dw_s = b_dw[0].reshape(d // LANES, 1, LANES)
    return _l1_post(h, u, mod1, w_dw_s, b_dw_s, ln_g[0:1], ln_b[0:1], w_pw2[0].astype(BF16), b_pw2[0:1],
                    g_ffn[1:2], wi_b, wd_b, g_final[None, :])
```

```python
import functools

import jax
import jax.numpy as jnp
import numpy as np
from jax import lax
from jax.experimental import pallas as pl
from jax.experimental.pallas import tpu as pltpu

F32 = jnp.float32
BF16 = jnp.bfloat16

HEAD_DIM = 64
N_Q_HEADS = 8
N_KV_HEADS = 2
GRID_W = 64
CHUNK = 128
N_SG_GROUPS = 4
ROPE_THETA = 10000.0
CONV_WIDTH = 31
EPS = 1e-6
Q_SCALE = HEAD_DIM ** -0.5 * 1.4426950408889634

LANES = 128
HALO = 16
FFN_CHUNK = 256
VMEM_LIMIT = 56 * 1024 * 1024

TM = 512
TQ = 512
TK = 256


def _const_spec(shape):
    nd = len(shape)
    return pl.BlockSpec(shape, lambda *_: (0,) * nd, pipeline_mode=pl.Buffered(1))


def _params(n_axes, flags=None):
    return pltpu.CompilerParams(dimension_semantics=("arbitrary",) * n_axes,
                                vmem_limit_bytes=VMEM_LIMIT, flags=flags)


def _dot(a, b):
    return jnp.dot(a, b, preferred_element_type=F32)


def _sigmoid(x):
    return 1.0 / (1.0 + jnp.exp(-x))


def _gelu_tanh(x):
    c = 0.7978845608028654
    return x * (0.5 * (1.0 + jnp.tanh(c * (x + 0.044715 * (x * x * x)))))


def _rms_mod(x, g, sc, sh):
    ms = jnp.mean(x * x, axis=-1, keepdims=True)
    return (x * lax.rsqrt(ms + EPS) * g) * (1.0 + sc) + sh


def _head_rms(z, bd, gain):
    z2 = z * z
    hi = z2.astype(BF16)
    lo = (z2 - hi.astype(F32)).astype(BF16)
    ssq = _dot(hi, bd) + _dot(lo, bd)
    return z * lax.rsqrt(ssq * (1.0 / HEAD_DIM) + EPS) * gain


def _rope(z, cos, s_up, s_dn):
    return z * cos + pltpu.roll(z, LANES - 16, 1) * s_up + pltpu.roll(z, 16, 1) * s_dn


def _adaln_kernel(c_ref, w_ref, b_ref, o_ref):
    c = c_ref[...]
    s = (c * _sigmoid(c)).astype(BF16)
    o_ref[...] = _dot(s, w_ref[...].astype(BF16)) + b_ref[...]


def _adaln(cvec, w_mod, b_mod):
    depth, d, n = w_mod.shape
    rows = cvec.shape[0]
    tn = 1536
    return pl.pallas_call(
        _adaln_kernel,
        grid=(depth, n // tn),
        in_specs=[
            pl.BlockSpec((rows, d), lambda l, j: (0, 0)),
            pl.BlockSpec((None, d, tn), lambda l, j: (l, 0, j)),
            pl.BlockSpec((None, 1, tn), lambda l, j: (l, 0, j)),
        ],
        out_specs=pl.BlockSpec((None, rows, tn), lambda l, j: (l, 0, j)),
        out_shape=jax.ShapeDtypeStruct((depth, rows, n), F32),
        compiler_params=_params(2),
        name="adaln",
    )(cvec, w_mod, b_mod.reshape(depth, 1, n))


def _l0_pre_kernel(x_ref, mod_ref, g_ref, w_ref, qg_ref, kg_ref, cos_ref, sup_ref, sdn_ref, bd_ref,
                   wsp_ref, bsp_ref, q_ref, k_ref, v_ref, sg_ref, pa_scr, pb_scr):
    i = pl.program_id(1)
    aw = N_Q_HEADS * HEAD_DIM
    kvw = N_KV_HEADS * HEAD_DIM
    sgw = N_SG_GROUPS * LANES
    o_u = aw + 2 * kvw
    o_v = o_u + sgw
    tm = x_ref.shape[0]
    nchunk = tm // CHUNK

    @pl.when(i == 0)
    def _zero():
        pb_scr[...] = jnp.zeros(pb_scr.shape, F32)

    def step(p_new, p_old):
        xm = _rms_mod(x_ref[...], g_ref[...], mod_ref[1:2, :], mod_ref[0:1, :]).astype(BF16)
        p_new[...] = _dot(xm, w_ref[...])
        bd = bd_ref[...]
        cos, sup, sdn = cos_ref[...], sup_ref[...], sdn_ref[...]
        for j in range(aw // LANES):
            qn = _head_rms(p_old[:, j * LANES:(j + 1) * LANES], bd, qg_ref[...])
            q_ref[j] = (_rope(qn, cos, sup, sdn) * Q_SCALE).T.astype(BF16)
        kn = _head_rms(p_old[:, aw:aw + kvw], bd, kg_ref[...])
        k_ref[...] = _rope(kn, cos, sup, sdn).astype(BF16)
        v_ref[...] = p_old[:, aw + kvw:aw + 2 * kvw].T.astype(BF16)
        for g in range(N_SG_GROUPS):
            gv = _gelu_tanh(p_old[:, o_v + g * LANES:o_v + (g + 1) * LANES])
            mu = jnp.mean(gv, axis=-1, keepdims=True)
            dv = gv - mu
            var = jnp.mean(dv * dv, axis=-1, keepdims=True)
            vn = (dv * lax.rsqrt(var + EPS)).astype(BF16)
            rhs = jnp.concatenate([vn[c * CHUNK:(c + 1) * CHUNK, :] for c in range(nchunk)], axis=1)
            mixed = _dot(wsp_ref[g], rhs)
            gu = _gelu_tanh(p_old[:, o_u + g * LANES:o_u + (g + 1) * LANES])
            for c in range(nchunk):
                blk = mixed[:, c * CHUNK:(c + 1) * CHUNK] + bsp_ref[g]
                sg_ref[c * CHUNK:(c + 1) * CHUNK, g * LANES:(g + 1) * LANES] = (
                    gu[c * CHUNK:(c + 1) * CHUNK, :] * blk).astype(BF16)

    @pl.when(lax.rem(i, 2) == 0)
    def _even():
        step(pa_scr, pb_scr)

    @pl.when(lax.rem(i, 2) == 1)
    def _odd():
        step(pb_scr, pa_scr)


def _l0_pre(x, mod, g_mix, w_in, qg2, kg2, cos, sup, sdn, bd, w_sp, bsp):
    b, s, d = x.shape
    inw = w_in.shape[1]
    npair = N_Q_HEADS * HEAD_DIM // LANES
    kvw = N_KV_HEADS * HEAD_DIM
    sgw = N_SG_GROUPS * LANES
    n_i = s // TM
    prev = lambda i: jnp.maximum(i - 1, 0)
    row = lambda bi, i: (bi, prev(i), 0)
    tab = pl.BlockSpec((TM, LANES), lambda bi, i: (prev(i), 0))
    return pl.pallas_call(
        _l0_pre_kernel,
        grid=(b, n_i + 1),
        in_specs=[
            pl.BlockSpec((None, TM, d), lambda bi, i: (bi, jnp.minimum(i, n_i - 1), 0)),
            pl.BlockSpec((None, 6, d), lambda bi, i: (bi, 0, 0)),
            _const_spec((1, d)),
            _const_spec((d, inw)),
            _const_spec((1, LANES)),
            _const_spec((1, LANES)),
            tab, tab, tab,
            _const_spec((LANES, LANES)),
            _const_spec((N_SG_GROUPS, CHUNK, CHUNK)),
            _const_spec((N_SG_GROUPS, CHUNK, LANES)),
        ],
        out_specs=[
            pl.BlockSpec((None, npair, LANES, TM), lambda bi, i: (bi, 0, 0, prev(i))),
            pl.BlockSpec((None, TM, kvw), row),
            pl.BlockSpec((None, kvw, TM), lambda bi, i: (bi, 0, prev(i))),
            pl.BlockSpec((None, TM, sgw), row),
        ],
        out_shape=[
            jax.ShapeDtypeStruct((b, npair, LANES, s), BF16),
            jax.ShapeDtypeStruct((b, s, kvw), BF16),
            jax.ShapeDtypeStruct((b, kvw, s), BF16),
            jax.ShapeDtypeStruct((b, s, sgw), BF16),
        ],
        scratch_shapes=[pltpu.VMEM((TM, inw), F32), pltpu.VMEM((TM, inw), F32)],
        compiler_params=_params(2),
        name="l0_pre",
    )(x, mod, g_mix, w_in, qg2, kg2, cos, sup, sdn, bd, w_sp, bsp)


def _ctx_kv_kernel(x_ref, mod_ref, g_ref, w_ref, kg_ref, bd_ref, k_ref, v_ref):
    kvw = N_KV_HEADS * HEAD_DIM
    xm = _rms_mod(x_ref[...], g_ref[...], mod_ref[1:2, :], mod_ref[0:1, :]).astype(BF16)
    p = _dot(xm, w_ref[...])
    k_ref[...] = _head_rms(p[:, 0:kvw], bd_ref[...], kg_ref[...]).astype(BF16)
    v_ref[...] = p[:, kvw:2 * kvw].T.astype(BF16)


def _ctx_kv(ctx, cmod, g_mix, w_kv, kg2, bd):
    b, m, d = ctx.shape
    kvw = N_KV_HEADS * HEAD_DIM
    row = lambda bi: (bi, 0, 0)
    return pl.pallas_call(
        _ctx_kv_kernel,
        grid=(b,),
        in_specs=[
            pl.BlockSpec((None, m, d), row),
            _const_spec((6, d)),
            _const_spec((1, d)),
            _const_spec((d, 2 * kvw)),
            _const_spec((1, LANES)),
            _const_spec((LANES, LANES)),
        ],
        out_specs=[pl.BlockSpec((None, m, kvw), row), pl.BlockSpec((None, kvw, m), row)],
        out_shape=[jax.ShapeDtypeStruct((b, m, kvw), BF16), jax.ShapeDtypeStruct((b, kvw, m), BF16)],
        compiler_params=_params(1),
        name="ctx_kv",
    )(ctx, cmod, g_mix, w_kv, kg2, bd)


NEG_INIT = -1e30
VX_ROWS = HEAD_DIM + 16


def _attn_kernel(q_ref, k_ref, kc_ref, v_ref, vc_ref, o_ref, ka_scr, kb_scr, vx_scr, s_scr, mx_scr, m_scr, acc_scr):
    h = pl.program_id(1)
    i = pl.program_id(2)
    tq = q_ref.shape[2]
    n_ctx = kc_ref.shape[0]
    n_keys = ka_scr.shape[0]
    chunks = [(r0, min(TK, n_ctx - r0)) for r0 in range(0, n_ctx, TK)]
    chunks += [(r0, TK) for r0 in range(n_ctx, n_keys, TK)]

    @pl.when(i == 0)
    def _fill():
        hrow = pl.ds(pl.multiple_of(h * HEAD_DIM, HEAD_DIM), HEAD_DIM)

        def put_k(kk, r0):
            lo = lax.broadcasted_iota(jnp.int32, kk.shape, 1) < HEAD_DIM
            kk = kk.astype(F32)
            ksw = pltpu.roll(kk, HEAD_DIM, 1)
            k_lo = jnp.where(h == 0, kk, ksw)
            k_hi = jnp.where(h == 0, ksw, kk)
            ka_scr[r0:r0 + kk.shape[0], :] = jnp.where(lo, k_lo, 0.0).astype(BF16)
            kb_scr[r0:r0 + kk.shape[0], :] = jnp.where(lo, 0.0, k_hi).astype(BF16)

        for r0, nk in chunks:
            if r0 < n_ctx:
                put_k(kc_ref[r0:r0 + nk, :], r0)
                vx_scr[0:HEAD_DIM, r0:r0 + nk] = vc_ref[hrow, r0:r0 + nk]
            else:
                put_k(k_ref[r0 - n_ctx:r0 - n_ctx + nk, :], r0)
                vx_scr[0:HEAD_DIM, r0:r0 + nk] = v_ref[hrow, r0 - n_ctx:r0 - n_ctx + nk]
            ones_row = lax.broadcasted_iota(jnp.int32, (VX_ROWS - HEAD_DIM, nk), 0) == 0
            vx_scr[HEAD_DIM:, r0:r0 + nk] = jnp.where(ones_row, 1.0, 0.0).astype(BF16)

    m_scr[...] = jnp.full(m_scr.shape, NEG_INIT, F32)
    acc_scr[...] = jnp.zeros(acc_scr.shape, F32)
    heads = range(m_scr.shape[0])

    def scores(c, g):
        r0, nk = chunks[c]
        k_scr = kb_scr if g % 2 else ka_scr
        s = _dot(k_scr[r0:r0 + nk, :], q_ref[g // 2])
        s_scr[c % 2, g, 0:nk, :] = s
        mx_scr[c % 2, g] = jnp.max(s, axis=0, keepdims=True)

    def softmax_pv(c, g):
        r0, nk = chunks[c]
        m_old = m_scr[g]
        m_new = jnp.maximum(m_old, mx_scr[c % 2, g])
        p = jnp.exp2(s_scr[c % 2, g, 0:nk, :] - m_new).astype(BF16)
        acc_scr[g] = jnp.exp2(m_old - m_new) * acc_scr[g] + _dot(vx_scr[:, r0:r0 + nk], p)
        m_scr[g] = m_new

    for g in heads:
        scores(0, g)
    for c in range(len(chunks)):
        for g in heads:
            if c + 1 < len(chunks):
                scores(c + 1, g)
            softmax_pv(c, g)

    for pair in range(len(heads) // 2):
        a = acc_scr[2 * pair]
        b = acc_scr[2 * pair + 1]
        st = jnp.concatenate([a[0:HEAD_DIM] / a[HEAD_DIM:HEAD_DIM + 1], b[0:HEAD_DIM] / b[HEAD_DIM:HEAD_DIM + 1]],
                             axis=0)
        o_ref[:, pair * LANES:(pair + 1) * LANES] = st.T.astype(BF16)


def _attention(q, k, kc, v, vc):
    b, npair, _, s = q.shape
    m = kc.shape[1]
    kvw = k.shape[2]
    pairs_per_kv = npair // N_KV_HEADS
    gw = pairs_per_kv * LANES
    assert s % TK == 0 and m % LANES == 0, (m, s, TK)
    full = lambda bi, h, i: (bi, 0, 0)
    return pl.pallas_call(
        _attn_kernel,
        grid=(b, N_KV_HEADS, s // TQ),
        in_specs=[
            pl.BlockSpec((None, pairs_per_kv, LANES, TQ), lambda bi, h, i: (bi, h, 0, i)),
            pl.BlockSpec((None, s, kvw), full),
            pl.BlockSpec((None, m, kvw), full),
            pl.BlockSpec((None, kvw, s), full),
            pl.BlockSpec((None, kvw, m), full),
        ],
        out_specs=pl.BlockSpec((None, TQ, gw), lambda bi, h, i: (bi, i, h)),
        out_shape=jax.ShapeDtypeStruct((b, s, N_KV_HEADS * gw), BF16),
        scratch_shapes=[
            pltpu.VMEM((m + s, LANES), BF16),
            pltpu.VMEM((m + s, LANES), BF16),
            pltpu.VMEM((VX_ROWS, m + s), BF16),
            pltpu.VMEM((2, 2 * pairs_per_kv, TK, TQ), F32),
            pltpu.VMEM((2, 2 * pairs_per_kv, 1, TQ), F32),
            pltpu.VMEM((2 * pairs_per_kv, 1, TQ), F32),
            pltpu.VMEM((2 * pairs_per_kv, VX_ROWS, TQ), F32),
        ],
        compiler_params=_params(3),
        name="attn",
    )(q, k, kc, v, vc)


def _ffn_residual(h, mod_ref, g_ref, wi_ref, wo_ref, xm_scr, acc_scr):
    dff = wo_ref.shape[0]
    xm_scr[...] = _rms_mod(h, g_ref[...], mod_ref[4:5, :], mod_ref[3:4, :]).astype(BF16)
    acc_scr[...] = jnp.zeros(acc_scr.shape, F32)

    for c0 in range(0, dff, FFN_CHUNK):
        xm = xm_scr[...]
        g = _dot(xm, wi_ref[:, c0:c0 + FFN_CHUNK])
        u = _dot(xm, wi_ref[:, dff + c0:dff + c0 + FFN_CHUNK])
        a = (g * _sigmoid(g) * u).astype(BF16)
        acc_scr[...] += _dot(a, wo_ref[c0:c0 + FFN_CHUNK, :])
    return h + mod_ref[5:6, :] * acc_scr[...]


def _layer_spec(w, layer):
    return pl.BlockSpec((None,) + w.shape[1:], lambda *_: (layer, 0, 0), pipeline_mode=pl.Buffered(1))


def _ffn_specs(d, wi, wd, layer):
    return [_const_spec((1, d)), _layer_spec(wi, layer), _layer_spec(wd, layer)]


def _ffn_scratch(d):
    return [pltpu.VMEM((TM, d), BF16), pltpu.VMEM((TM, d), F32)]


def _l0_post_kernel(x_ref, a_ref, sg_ref, mod_ref, wo_ref, gf_ref, wi_ref, wd_ref, mod1_ref, g1_ref, w1_ref, b1_ref,
                    h_ref, u_ref, xm_scr, acc_scr):
    aw = a_ref.shape[1]
    d = x_ref.shape[1]
    y = _dot(a_ref[...], wo_ref[0:aw, :]) + _dot(sg_ref[...], wo_ref[aw:, :])
    h = x_ref[...] + mod_ref[2:3, :] * y
    h = _ffn_residual(h, mod_ref, gf_ref, wi_ref, wd_ref, xm_scr, acc_scr)
    h_ref[...] = h
    xm_scr[...] = _rms_mod(h, g1_ref[...], mod1_ref[1:2, :], mod1_ref[0:1, :]).astype(BF16)
    for c0 in range(0, d, FFN_CHUNK):
        xm = xm_scr[...]
        a = _dot(xm, w1_ref[:, c0:c0 + FFN_CHUNK]) + b1_ref[:, c0:c0 + FFN_CHUNK]
        gate = _dot(xm, w1_ref[:, d + c0:d + c0 + FFN_CHUNK]) + b1_ref[:, d + c0:d + c0 + FFN_CHUNK]
        u_ref[:, c0:c0 + FFN_CHUNK] = a * _sigmoid(gate)


def _l0_post(x, attn, sg, mod, w_out, g_ffn, wi, wd, mod1, g_mix1, w_pw1, b_pw1):
    b, s, d = x.shape
    row = lambda bi, i: (bi, i, 0)
    mods = pl.BlockSpec((None, 6, d), lambda bi, i: (bi, 0, 0))
    tile = pl.BlockSpec((None, TM, d), row)
    return pl.pallas_call(
        _l0_post_kernel,
        grid=(b, s // TM),
        in_specs=[
            tile,
            pl.BlockSpec((None, TM, attn.shape[2]), row),
            pl.BlockSpec((None, TM, sg.shape[2]), row),
            mods,
            _const_spec(w_out.shape),
        ] + _ffn_specs(d, wi, wd, 0) + [mods, _const_spec((1, d)), _const_spec(w_pw1.shape), _const_spec((1, 2 * d))],
        out_specs=[tile, tile],
        out_shape=[jax.ShapeDtypeStruct((b, s, d), F32), jax.ShapeDtypeStruct((b, s, d), F32)],
        scratch_shapes=_ffn_scratch(d),
        compiler_params=_params(2),
        name="l0_post",
    )(x, attn, sg, mod, w_out, g_ffn, wi, wd, mod1, g_mix1, w_pw1, b_pw1)


def _l1_post_kernel(h_ref, u_ref, up_ref, un_ref, mod_ref, wdw_ref, bdw_ref, lng_ref, lnb_ref, w2_ref, b2_ref,
                    gf_ref, wi_ref, wd_ref, gfin_ref, o_ref, e_scr, c_scr, xm_scr, acc_scr):
    i = pl.program_id(1)
    n_i = pl.num_programs(1)
    tm = h_ref.shape[0]
    nslab = e_scr.shape[0]
    up = jnp.where(i > 0, up_ref[...], 0.0)
    un = jnp.where(i < n_i - 1, un_ref[...], 0.0)
    for sl in range(nslab):
        cols = slice(sl * LANES, (sl + 1) * LANES)
        e_scr[sl, 0:HALO, :] = up[:, cols]
        e_scr[sl, HALO:HALO + tm, :] = u_ref[:, cols]
        e_scr[sl, HALO + tm:, :] = un[:, cols]
    rb = 64
    off = HALO - CONV_WIDTH // 2

    def conv_slab(sl, carry):
        for r0 in range(0, tm, rb):
            acc = jnp.broadcast_to(bdw_ref[sl], (rb, LANES))
            for j in range(CONV_WIDTH):
                acc = acc + wdw_ref[sl, j:j + 1, :] * e_scr[sl, r0 + off + j:r0 + off + j + rb, :]
            c_scr[sl, r0:r0 + rb, :] = acc
        return carry

    lax.fori_loop(0, nslab, conv_slab, 0)
    cv = jnp.concatenate([c_scr[sl] for sl in range(nslab)], axis=1)
    mu = jnp.mean(cv, axis=-1, keepdims=True)
    dv = cv - mu
    var = jnp.mean(dv * dv, axis=-1, keepdims=True)
    ln = dv * lax.rsqrt(var + EPS) * lng_ref[...] + lnb_ref[...]
    act = (ln * _sigmoid(ln)).astype(BF16)
    y = _dot(act, w2_ref[...]) + b2_ref[...]
    h = h_ref[...] + mod_ref[2:3, :] * y
    h = _ffn_residual(h, mod_ref, gf_ref, wi_ref, wd_ref, xm_scr, acc_scr)
    ms = jnp.mean(h * h, axis=-1, keepdims=True)
    o_ref[...] = h * lax.rsqrt(ms + EPS) * gfin_ref[...]


def _l1_post(h, u, mod, w_dw, b_dw, ln_g, ln_b, w_pw2, b_pw2, g_ffn, wi, wd, g_final):
    b, s, d = h.shape
    row = lambda bi, i: (bi, i, 0)
    per = TM // HALO
    last = s // HALO - 1
    vec = _const_spec((1, d))
    return pl.pallas_call(
        _l1_post_kernel,
        grid=(b, s // TM),
        in_specs=[
            pl.BlockSpec((None, TM, d), row),
            pl.BlockSpec((None, TM, d), row),
            pl.BlockSpec((None, HALO, d), lambda bi, i: (bi, jnp.maximum(i * per - 1, 0), 0)),
            pl.BlockSpec((None, HALO, d), lambda bi, i: (bi, jnp.minimum((i + 1) * per, last), 0)),
            pl.BlockSpec((None, 6, d), lambda bi, i: (bi, 0, 0)),
            _const_spec(w_dw.shape), _const_spec(b_dw.shape), vec, vec,
            _const_spec(w_pw2.shape), vec,
        ] + _ffn_specs(d, wi, wd, 1) + [vec],
        out_specs=pl.BlockSpec((None, TM, d), row),
        out_shape=jax.ShapeDtypeStruct((b, s, d), F32),
        scratch_shapes=[pltpu.VMEM((d // LANES, TM + 2 * HALO, LANES), F32),
                        pltpu.VMEM((d // LANES, TM, LANES), F32)] + _ffn_scratch(d),
        compiler_params=_params(2),
        name="l1_post",
    )(h, u, u, u, mod, w_dw, b_dw, ln_g, ln_b, w_pw2, b_pw2, g_ffn, wi, wd, g_final)


def _rope_tables(n):
    half = HEAD_DIM // 2
    pos = np.arange(n)
    row = (pos // GRID_W).astype(np.float32)
    col = (pos % GRID_W).astype(np.float32)
    inv = (ROPE_THETA ** (-np.arange(0, half, 2, dtype=np.float32) / half)).astype(np.float32)
    lane = np.arange(LANES)
    hd = lane % HEAD_DIM
    w = hd % half
    ang = np.where((hd < half)[None, :], row[:, None], col[:, None]) * inv[w % (half // 2)][None, :]
    first = (w < half // 2)[None, :]
    sin = np.sin(ang)
    tabs = (np.cos(ang), np.where(first, -sin, 0.0), np.where(first, 0.0, sin))
    return tuple(jnp.asarray(t, F32) for t in tabs)


def kernel(x, c, ctx, c_ctx, w_mod, b_mod, g_mix, g_ffn, w_ffn_in, w_ffn_out, w_in, q_gain, k_gain, w_sp, b_sp,
           w_out, w_pw1, b_pw1, w_dw, b_dw, ln_g, ln_b, w_pw2, b_pw2, g_final):
    b, s, d = x.shape
    aw = N_Q_HEADS * HEAD_DIM
    kvw = N_KV_HEADS * HEAD_DIM

    rows = -(-(b + 1) // 8) * 8
    cvec = jnp.concatenate([c, c_ctx[None, :], jnp.zeros((rows - b - 1, d), F32)], axis=0)
    mod = _adaln(cvec, w_mod, b_mod)
    mod0 = mod[0, :b].reshape(b, 6, d)
    cmod0 = mod[0, b].reshape(6, d)
    mod1 = mod[1, :b].reshape(b, 6, d)

    cos, sup, sdn = _rope_tables(s)
    lane = jnp.arange(LANES)
    bd = (lane[:, None] // HEAD_DIM == lane[None, :] // HEAD_DIM).astype(BF16)
    qg2 = jnp.tile(q_gain[0], LANES // HEAD_DIM)[None, :]
    kg2 = jnp.tile(k_gain[0], LANES // HEAD_DIM)[None, :]
    w_in_b = w_in[0].astype(BF16)
    bsp = jnp.broadcast_to(b_sp[0][:, :, None], (N_SG_GROUPS, CHUNK, LANES))

    q, k, v, sg = _l0_pre(x, mod0, g_mix[0:1], w_in_b, qg2, kg2, cos, sup, sdn, bd, w_sp[0].astype(BF16), bsp)
    kc, vc = _ctx_kv(ctx, cmod0, g_mix[0:1], w_in_b[:, aw:aw + 2 * kvw], kg2, bd)
    attn = _attention(q, k, kc, v, vc)

    assert w_ffn_out.shape[1] % FFN_CHUNK == 0 and d % FFN_CHUNK == 0, (w_ffn_out.shape, d)
    wi_b = w_ffn_in.astype(BF16)
    wd_b = w_ffn_out.astype(BF16)
    h, u = _l0_post(x, attn, sg, mod0, w_out[0].astype(BF16), g_ffn[0:1], wi_b, wd_b,
                    mod1, g_mix[1:2], w_pw1[0].astype(BF16), b_pw1[0:1])
    w_dw_s = w_dw[0].reshape(CONV_WIDTH, d // LANES, LANES).transpose(1, 0, 2)
    b_dw_s = b_dw[0].reshape(d // LANES, 1, LANES)
    return _l1_post(h, u, mod1, w_dw_s, b_dw_s, ln_g[0:1], ln_b[0:1], w_pw2[0].astype(BF16), b_pw2[0:1],
                    g_ffn[1:2], wi_b, wd_b, g_final[None, :])
```

```python
import functools

import jax
import jax.numpy as jnp
import numpy as np
from jax import lax
from jax.experimental import pallas as pl
from jax.experimental.pallas import tpu as pltpu

F32 = jnp.float32
BF16 = jnp.bfloat16

HEAD_DIM = 64
N_Q_HEADS = 8
N_KV_HEADS = 2
GRID_W = 64
CHUNK = 128
N_SG_GROUPS = 4
ROPE_THETA = 10000.0
CONV_WIDTH = 31
EPS = 1e-6
Q_SCALE = HEAD_DIM ** -0.5 * 1.4426950408889634

LANES = 128
HALO = 16
FFN_CHUNK = 256
VMEM_LIMIT = 56 * 1024 * 1024

TM = 512
TQ = 512
TK = 256


def _const_spec(shape):
    nd = len(shape)
    return pl.BlockSpec(shape, lambda *_: (0,) * nd, pipeline_mode=pl.Buffered(1))


def _params(n_axes, flags=None):
    return pltpu.CompilerParams(dimension_semantics=("arbitrary",) * n_axes,
                                vmem_limit_bytes=VMEM_LIMIT, flags=flags)


def _dot(a, b):
    return jnp.dot(a, b, preferred_element_type=F32)


def _sigmoid(x):
    return 1.0 / (1.0 + jnp.exp(-x))


def _gelu_tanh(x):
    c = 0.7978845608028654
    return x * (0.5 * (1.0 + jnp.tanh(c * (x + 0.044715 * (x * x * x)))))


def _rms_mod(x, g, sc, sh):
    ms = jnp.mean(x * x, axis=-1, keepdims=True)
    return (x * lax.rsqrt(ms + EPS) * g) * (1.0 + sc) + sh


def _head_rms(z, bd, gain):
    z2 = z * z
    hi = z2.astype(BF16)
    lo = (z2 - hi.astype(F32)).astype(BF16)
    ssq = _dot(hi, bd) + _dot(lo, bd)
    return z * lax.rsqrt(ssq * (1.0 / HEAD_DIM) + EPS) * gain


def _rope(z, cos, s_up, s_dn):
    return z * cos + pltpu.roll(z, LANES - 16, 1) * s_up + pltpu.roll(z, 16, 1) * s_dn


def _adaln_kernel(c_ref, w_ref, b_ref, o_ref):
    c = c_ref[...]
    s = (c * _sigmoid(c)).astype(BF16)
    o_ref[...] = _dot(s, w_ref[...].astype(BF16)) + b_ref[...]


def _adaln(cvec, w_mod, b_mod):
    depth, d, n = w_mod.shape
    rows = cvec.shape[0]
    tn = 1536
    return pl.pallas_call(
        _adaln_kernel,
        grid=(depth, n // tn),
        in_specs=[
            pl.BlockSpec((rows, d), lambda l, j: (0, 0)),
            pl.BlockSpec((None, d, tn), lambda l, j: (l, 0, j)),
            pl.BlockSpec((None, 1, tn), lambda l, j: (l, 0, j)),
        ],
        out_specs=pl.BlockSpec((None, rows, tn), lambda l, j: (l, 0, j)),
        out_shape=jax.ShapeDtypeStruct((depth, rows, n), F32),
        compiler_params=_params(2),
        name="adaln",
    )(cvec, w_mod, b_mod.reshape(depth, 1, n))


def _l0_pre_kernel(x_ref, mod_ref, g_ref, w_ref, qg_ref, kg_ref, cos_ref, sup_ref, sdn_ref, bd_ref,
                   wsp_ref, bsp_ref, q_ref, k_ref, v_ref, sg_ref, pa_scr, pb_scr):
    i = pl.program_id(1)
    aw = N_Q_HEADS * HEAD_DIM
    kvw = N_KV_HEADS * HEAD_DIM
    sgw = N_SG_GROUPS * LANES
    o_u = aw + 2 * kvw
    o_v = o_u + sgw
    tm = x_ref.shape[0]
    nchunk = tm // CHUNK

    @pl.when(i == 0)
    def _zero():
        pb_scr[...] = jnp.zeros(pb_scr.shape, F32)

    def step(p_new, p_old):
        xm = _rms_mod(x_ref[...], g_ref[...], mod_ref[1:2, :], mod_ref[0:1, :]).astype(BF16)
        p_new[...] = _dot(xm, w_ref[...])
        bd = bd_ref[...]
        cos, sup, sdn = cos_ref[...], sup_ref[...], sdn_ref[...]
        for j in range(aw // LANES):
            qn = _head_rms(p_old[:, j * LANES:(j + 1) * LANES], bd, qg_ref[...])
            q_ref[j] = (_rope(qn, cos, sup, sdn) * Q_SCALE).T.astype(BF16)
        kn = _head_rms(p_old[:, aw:aw + kvw], bd, kg_ref[...])
        k_ref[...] = _rope(kn, cos, sup, sdn).astype(BF16)
        v_ref[...] = p_old[:, aw + kvw:aw + 2 * kvw].T.astype(BF16)
        for g in range(N_SG_GROUPS):
            gv = _gelu_tanh(p_old[:, o_v + g * LANES:o_v + (g + 1) * LANES])
            mu = jnp.mean(gv, axis=-1, keepdims=True)
            dv = gv - mu
            var = jnp.mean(dv * dv, axis=-1, keepdims=True)
            vn = (dv * lax.rsqrt(var + EPS)).astype(BF16)
            rhs = jnp.concatenate([vn[c * CHUNK:(c + 1) * CHUNK, :] for c in range(nchunk)], axis=1)
            mixed = _dot(wsp_ref[g], rhs)
            gu = _gelu_tanh(p_old[:, o_u + g * LANES:o_u + (g + 1) * LANES])
            for c in range(nchunk):
                blk = mixed[:, c * CHUNK:(c + 1) * CHUNK] + bsp_ref[g]
                sg_ref[c * CHUNK:(c + 1) * CHUNK, g * LANES:(g + 1) * LANES] = (
                    gu[c * CHUNK:(c + 1) * CHUNK, :] * blk).astype(BF16)

    @pl.when(lax.rem(i, 2) == 0)
    def _even():
        step(pa_scr, pb_scr)

    @pl.when(lax.rem(i, 2) == 1)
    def _odd():
        step(pb_scr, pa_scr)


def _l0_pre(x, mod, g_mix, w_in, qg2, kg2, cos, sup, sdn, bd, w_sp, bsp):
    b, s, d = x.shape
    inw = w_in.shape[1]
    npair = N_Q_HEADS * HEAD_DIM // LANES
    kvw = N_KV_HEADS * HEAD_DIM
    sgw = N_SG_GROUPS * LANES
    n_i = s // TM
    prev = lambda i: jnp.maximum(i - 1, 0)
    row = lambda bi, i: (bi, prev(i), 0)
    tab = pl.BlockSpec((TM, LANES), lambda bi, i: (prev(i), 0))
    return pl.pallas_call(
        _l0_pre_kernel,
        grid=(b, n_i + 1),
        in_specs=[
            pl.BlockSpec((None, TM, d), lambda bi, i: (bi, jnp.minimum(i, n_i - 1), 0)),
            pl.BlockSpec((None, 6, d), lambda bi, i: (bi, 0, 0)),
            _const_spec((1, d)),
            _const_spec((d, inw)),
            _const_spec((1, LANES)),
            _const_spec((1, LANES)),
            tab, tab, tab,
            _const_spec((LANES, LANES)),
            _const_spec((N_SG_GROUPS, CHUNK, CHUNK)),
            _const_spec((N_SG_GROUPS, CHUNK, LANES)),
        ],
        out_specs=[
            pl.BlockSpec((None, npair, LANES, TM), lambda bi, i: (bi, 0, 0, prev(i))),
            pl.BlockSpec((None, TM, kvw), row),
            pl.BlockSpec((None, kvw, TM), lambda bi, i: (bi, 0, prev(i))),
            pl.BlockSpec((None, TM, sgw), row),
        ],
        out_shape=[
            jax.ShapeDtypeStruct((b, npair, LANES, s), BF16),
            jax.ShapeDtypeStruct((b, s, kvw), BF16),
            jax.ShapeDtypeStruct((b, kvw, s), BF16),
            jax.ShapeDtypeStruct((b, s, sgw), BF16),
        ],
        scratch_shapes=[pltpu.VMEM((TM, inw), F32), pltpu.VMEM((TM, inw), F32)],
        compiler_params=_params(2),
        name="l0_pre",
    )(x, mod, g_mix, w_in, qg2, kg2, cos, sup, sdn, bd, w_sp, bsp)


def _ctx_kv_kernel(x_ref, mod_ref, g_ref, w_ref, kg_ref, bd_ref, k_ref, v_ref):
    kvw = N_KV_HEADS * HEAD_DIM
    xm = _rms_mod(x_ref[...], g_ref[...], mod_ref[1:2, :], mod_ref[0:1, :]).astype(BF16)
    p = _dot(xm, w_ref[...])
    k_ref[...] = _head_rms(p[:, 0:kvw], bd_ref[...], kg_ref[...]).astype(BF16)
    v_ref[...] = p[:, kvw:2 * kvw].T.astype(BF16)


def _ctx_kv(ctx, cmod, g_mix, w_kv, kg2, bd):
    b, m, d = ctx.shape
    kvw = N_KV_HEADS * HEAD_DIM
    row = lambda bi: (bi, 0, 0)
    return pl.pallas_call(
        _ctx_kv_kernel,
        grid=(b,),
        in_specs=[
            pl.BlockSpec((None, m, d), row),
            _const_spec((6, d)),
            _const_spec((1, d)),
            _const_spec((d, 2 * kvw)),
            _const_spec((1, LANES)),
            _const_spec((LANES, LANES)),
        ],
        out_specs=[pl.BlockSpec((None, m, kvw), row), pl.BlockSpec((None, kvw, m), row)],
        out_shape=[jax.ShapeDtypeStruct((b, m, kvw), BF16), jax.ShapeDtypeStruct((b, kvw, m), BF16)],
        compiler_params=_params(1),
        name="ctx_kv",
    )(ctx, cmod, g_mix, w_kv, kg2, bd)


NEG_INIT = -1e30
VX_ROWS = HEAD_DIM + 16
S_SLOTS = 2


def _attn_kernel(q_ref, k_ref, kc_ref, v_ref, vc_ref, o_ref, ka_scr, kb_scr, vx_scr, s_scr, mx_scr, m_scr, acc_scr):
    h = pl.program_id(1)
    i = pl.program_id(2)
    tq = q_ref.shape[2]
    n_ctx = kc_ref.shape[0]
    n_keys = ka_scr.shape[0]
    chunks = [(r0, min(TK, n_ctx - r0)) for r0 in range(0, n_ctx, TK)]
    chunks += [(r0, TK) for r0 in range(n_ctx, n_keys, TK)]

    @pl.when(i == 0)
    def _fill():
        hrow = pl.ds(pl.multiple_of(h * HEAD_DIM, HEAD_DIM), HEAD_DIM)

        def put_k(kk, r0):
            lo = lax.broadcasted_iota(jnp.int32, kk.shape, 1) < HEAD_DIM
            kk = kk.astype(F32)
            ksw = pltpu.roll(kk, HEAD_DIM, 1)
            k_lo = jnp.where(h == 0, kk, ksw)
            k_hi = jnp.where(h == 0, ksw, kk)
            ka_scr[r0:r0 + kk.shape[0], :] = jnp.where(lo, k_lo, 0.0).astype(BF16)
            kb_scr[r0:r0 + kk.shape[0], :] = jnp.where(lo, 0.0, k_hi).astype(BF16)

        for r0, nk in chunks:
            if r0 < n_ctx:
                put_k(kc_ref[r0:r0 + nk, :], r0)
                vx_scr[0:HEAD_DIM, r0:r0 + nk] = vc_ref[hrow, r0:r0 + nk]
            else:
                put_k(k_ref[r0 - n_ctx:r0 - n_ctx + nk, :], r0)
                vx_scr[0:HEAD_DIM, r0:r0 + nk] = v_ref[hrow, r0 - n_ctx:r0 - n_ctx + nk]
            ones_row = lax.broadcasted_iota(jnp.int32, (VX_ROWS - HEAD_DIM, nk), 0) == 0
            vx_scr[HEAD_DIM:, r0:r0 + nk] = jnp.where(ones_row, 1.0, 0.0).astype(BF16)

    m_scr[...] = jnp.full(m_scr.shape, NEG_INIT, F32)
    acc_scr[...] = jnp.zeros(acc_scr.shape, F32)
    heads = range(m_scr.shape[0])

    def scores(c, g):
        r0, nk = chunks[c]
        k_scr = kb_scr if g % 2 else ka_scr
        s = _dot(k_scr[r0:r0 + nk, :], q_ref[g // 2])
        s_scr[c % S_SLOTS, g, 0:nk, :] = s
        mx_scr[c % S_SLOTS, g] = jnp.max(s, axis=0, keepdims=True)

    def softmax_pv(c, g):
        r0, nk = chunks[c]
        m_old = m_scr[g]
        m_new = jnp.maximum(m_old, mx_scr[c % S_SLOTS, g])
        p = jnp.exp2(s_scr[c % S_SLOTS, g, 0:nk, :] - m_new).astype(BF16)
        acc_scr[g] = jnp.exp2(m_old - m_new) * acc_scr[g] + _dot(vx_scr[:, r0:r0 + nk], p)
        m_scr[g] = m_new

    ahead = S_SLOTS - 1
    for c in range(min(ahead, len(chunks))):
        for g in heads:
            scores(c, g)
    for c in range(len(chunks)):
        for g in heads:
            softmax_pv(c, g)
            if c + ahead < len(chunks):
                scores(c + ahead, g)

    for pair in range(len(heads) // 2):
        a = acc_scr[2 * pair]
        b = acc_scr[2 * pair + 1]
        st = jnp.concatenate([a[0:HEAD_DIM] / a[HEAD_DIM:HEAD_DIM + 1], b[0:HEAD_DIM] / b[HEAD_DIM:HEAD_DIM + 1]],
                             axis=0)
        o_ref[:, pair * LANES:(pair + 1) * LANES] = st.T.astype(BF16)


def _attention(q, k, kc, v, vc):
    b, npair, _, s = q.shape
    m = kc.shape[1]
    kvw = k.shape[2]
    pairs_per_kv = npair // N_KV_HEADS
    gw = pairs_per_kv * LANES
    assert s % TK == 0 and m % LANES == 0, (m, s, TK)
    full = lambda bi, h, i: (bi, 0, 0)
    return pl.pallas_call(
        _attn_kernel,
        grid=(b, N_KV_HEADS, s // TQ),
        in_specs=[
            pl.BlockSpec((None, pairs_per_kv, LANES, TQ), lambda bi, h, i: (bi, h, 0, i)),
            pl.BlockSpec((None, s, kvw), full),
            pl.BlockSpec((None, m, kvw), full),
            pl.BlockSpec((None, kvw, s), full),
            pl.BlockSpec((None, kvw, m), full),
        ],
        out_specs=pl.BlockSpec((None, TQ, gw), lambda bi, h, i: (bi, i, h)),
        out_shape=jax.ShapeDtypeStruct((b, s, N_KV_HEADS * gw), BF16),
        scratch_shapes=[
            pltpu.VMEM((m + s, LANES), BF16),
            pltpu.VMEM((m + s, LANES), BF16),
            pltpu.VMEM((VX_ROWS, m + s), BF16),
            pltpu.VMEM((S_SLOTS, 2 * pairs_per_kv, TK, TQ), F32),
            pltpu.VMEM((S_SLOTS, 2 * pairs_per_kv, 1, TQ), F32),
            pltpu.VMEM((2 * pairs_per_kv, 1, TQ), F32),
            pltpu.VMEM((2 * pairs_per_kv, VX_ROWS, TQ), F32),
        ],
        compiler_params=_params(3),
        name="attn",
    )(q, k, kc, v, vc)


def _ffn_residual(h, mod_ref, g_ref, wi_ref, wo_ref, xm_scr, acc_scr):
    dff = wo_ref.shape[0]
    xm_scr[...] = _rms_mod(h, g_ref[...], mod_ref[4:5, :], mod_ref[3:4, :]).astype(BF16)
    acc_scr[...] = jnp.zeros(acc_scr.shape, F32)

    for c0 in range(0, dff, FFN_CHUNK):
        xm = xm_scr[...]
        g = _dot(xm, wi_ref[:, c0:c0 + FFN_CHUNK])
        u = _dot(xm, wi_ref[:, dff + c0:dff + c0 + FFN_CHUNK])
        a = (g * _sigmoid(g) * u).astype(BF16)
        acc_scr[...] += _dot(a, wo_ref[c0:c0 + FFN_CHUNK, :])
    return h + mod_ref[5:6, :] * acc_scr[...]


def _layer_spec(w, layer):
    return pl.BlockSpec((None,) + w.shape[1:], lambda *_: (layer, 0, 0), pipeline_mode=pl.Buffered(1))


def _ffn_specs(d, wi, wd, layer):
    return [_const_spec((1, d)), _layer_spec(wi, layer), _layer_spec(wd, layer)]


def _ffn_scratch(d):
    return [pltpu.VMEM((TM, d), BF16), pltpu.VMEM((TM, d), F32)]


def _l0_post_kernel(x_ref, a_ref, sg_ref, mod_ref, wo_ref, gf_ref, wi_ref, wd_ref, mod1_ref, g1_ref, w1_ref, b1_ref,
                    h_ref, u_ref, xm_scr, acc_scr):
    aw = a_ref.shape[1]
    d = x_ref.shape[1]
    y = _dot(a_ref[...], wo_ref[0:aw, :]) + _dot(sg_ref[...], wo_ref[aw:, :])
    h = x_ref[...] + mod_ref[2:3, :] * y
    h = _ffn_residual(h, mod_ref, gf_ref, wi_ref, wd_ref, xm_scr, acc_scr)
    h_ref[...] = h
    xm_scr[...] = _rms_mod(h, g1_ref[...], mod1_ref[1:2, :], mod1_ref[0:1, :]).astype(BF16)
    for c0 in range(0, d, FFN_CHUNK):
        xm = xm_scr[...]
        a = _dot(xm, w1_ref[:, c0:c0 + FFN_CHUNK]) + b1_ref[:, c0:c0 + FFN_CHUNK]
        gate = _dot(xm, w1_ref[:, d + c0:d + c0 + FFN_CHUNK]) + b1_ref[:, d + c0:d + c0 + FFN_CHUNK]
        u_ref[:, c0:c0 + FFN_CHUNK] = a * _sigmoid(gate)


def _l0_post(x, attn, sg, mod, w_out, g_ffn, wi, wd, mod1, g_mix1, w_pw1, b_pw1):
    b, s, d = x.shape
    row = lambda bi, i: (bi, i, 0)
    mods = pl.BlockSpec((None, 6, d), lambda bi, i: (bi, 0, 0))
    tile = pl.BlockSpec((None, TM, d), row)
    return pl.pallas_call(
        _l0_post_kernel,
        grid=(b, s // TM),
        in_specs=[
            tile,
            pl.BlockSpec((None, TM, attn.shape[2]), row),
            pl.BlockSpec((None, TM, sg.shape[2]), row),
            mods,
            _const_spec(w_out.shape),
        ] + _ffn_specs(d, wi, wd, 0) + [mods, _const_spec((1, d)), _const_spec(w_pw1.shape), _const_spec((1, 2 * d))],
        out_specs=[tile, tile],
        out_shape=[jax.ShapeDtypeStruct((b, s, d), F32), jax.ShapeDtypeStruct((b, s, d), F32)],
        scratch_shapes=_ffn_scratch(d),
        compiler_params=_params(2),
        name="l0_post",
    )(x, attn, sg, mod, w_out, g_ffn, wi, wd, mod1, g_mix1, w_pw1, b_pw1)


def _l1_post_kernel(h_ref, u_ref, up_ref, un_ref, mod_ref, wdw_ref, bdw_ref, lng_ref, lnb_ref, w2_ref, b2_ref,
                    gf_ref, wi_ref, wd_ref, gfin_ref, o_ref, e_scr, c_scr, xm_scr, acc_scr):
    i = pl.program_id(1)
    n_i = pl.num_programs(1)
    tm = h_ref.shape[0]
    nslab = e_scr.shape[0]
    up = jnp.where(i > 0, up_ref[...], 0.0)
    un = jnp.where(i < n_i - 1, un_ref[...], 0.0)
    for sl in range(nslab):
        cols = slice(sl * LANES, (sl + 1) * LANES)
        e_scr[sl, 0:HALO, :] = up[:, cols]
        e_scr[sl, HALO:HALO + tm, :] = u_ref[:, cols]
        e_scr[sl, HALO + tm:, :] = un[:, cols]
    rb = 64
    off = HALO - CONV_WIDTH // 2

    def conv_slab(sl, carry):
        for r0 in range(0, tm, rb):
            acc = jnp.broadcast_to(bdw_ref[sl], (rb, LANES))
            for j in range(CONV_WIDTH):
                acc = acc + wdw_ref[sl, j:j + 1, :] * e_scr[sl, r0 + off + j:r0 + off + j + rb, :]
            c_scr[sl, r0:r0 + rb, :] = acc
        return carry

    lax.fori_loop(0, nslab, conv_slab, 0)
    cv = jnp.concatenate([c_scr[sl] for sl in range(nslab)], axis=1)
    mu = jnp.mean(cv, axis=-1, keepdims=True)
    dv = cv - mu
    var = jnp.mean(dv * dv, axis=-1, keepdims=True)
    ln = dv * lax.rsqrt(var + EPS) * lng_ref[...] + lnb_ref[...]
    act = (ln * _sigmoid(ln)).astype(BF16)
    y = _dot(act, w2_ref[...]) + b2_ref[...]
    h = h_ref[...] + mod_ref[2:3, :] * y
    h = _ffn_residual(h, mod_ref, gf_ref, wi_ref, wd_ref, xm_scr, acc_scr)
    ms = jnp.mean(h * h, axis=-1, keepdims=True)
    o_ref[...] = h * lax.rsqrt(ms + EPS) * gfin_ref[...]


def _l1_post(h, u, mod, w_dw, b_dw, ln_g, ln_b, w_pw2, b_pw2, g_ffn, wi, wd, g_final):
    b, s, d = h.shape
    row = lambda bi, i: (bi, i, 0)
    per = TM // HALO
    last = s // HALO - 1
    vec = _const_spec((1, d))
    return pl.pallas_call(
        _l1_post_kernel,
        grid=(b, s // TM),
        in_specs=[
            pl.BlockSpec((None, TM, d), row),
            pl.BlockSpec((None, TM, d), row),
            pl.BlockSpec((None, HALO, d), lambda bi, i: (bi, jnp.maximum(i * per - 1, 0), 0)),
            pl.BlockSpec((None, HALO, d), lambda bi, i: (bi, jnp.minimum((i + 1) * per, last), 0)),
            pl.BlockSpec((None, 6, d), lambda bi, i: (bi, 0, 0)),
            _const_spec(w_dw.shape), _const_spec(b_dw.shape), vec, vec,
            _const_spec(w_pw2.shape), vec,
        ] + _ffn_specs(d, wi, wd, 1) + [vec],
        out_specs=pl.BlockSpec((None, TM, d), row),
        out_shape=jax.ShapeDtypeStruct((b, s, d), F32),
        scratch_shapes=[pltpu.VMEM((d // LANES, TM + 2 * HALO, LANES), F32),
                        pltpu.VMEM((d // LANES, TM, LANES), F32)] + _ffn_scratch(d),
        compiler_params=_params(2),
        name="l1_post",
    )(h, u, u, u, mod, w_dw, b_dw, ln_g, ln_b, w_pw2, b_pw2, g_ffn, wi, wd, g_final)


def _rope_tables(n):
    half = HEAD_DIM // 2
    pos = np.arange(n)
    row = (pos // GRID_W).astype(np.float32)
    col = (pos % GRID_W).astype(np.float32)
    inv = (ROPE_THETA ** (-np.arange(0, half, 2, dtype=np.float32) / half)).astype(np.float32)
    lane = np.arange(LANES)
    hd = lane % HEAD_DIM
    w = hd % half
    ang = np.where((hd < half)[None, :], row[:, None], col[:, None]) * inv[w % (half // 2)][None, :]
    first = (w < half // 2)[None, :]
    sin = np.sin(ang)
    tabs = (np.cos(ang), np.where(first, -sin, 0.0), np.where(first, 0.0, sin))
    return tuple(jnp.asarray(t, F32) for t in tabs)


def kernel(x, c, ctx, c_ctx, w_mod, b_mod, g_mix, g_ffn, w_ffn_in, w_ffn_out, w_in, q_gain, k_gain, w_sp, b_sp,
           w_out, w_pw1, b_pw1, w_dw, b_dw, ln_g, ln_b, w_pw2, b_pw2, g_final):
    b, s, d = x.shape
    aw = N_Q_HEADS * HEAD_DIM
    kvw = N_KV_HEADS * HEAD_DIM

    rows = -(-(b + 1) // 8) * 8
    cvec = jnp.concatenate([c, c_ctx[None, :], jnp.zeros((rows - b - 1, d), F32)], axis=0)
    mod = _adaln(cvec, w_mod, b_mod)
    mod0 = mod[0, :b].reshape(b, 6, d)
    cmod0 = mod[0, b].reshape(6, d)
    mod1 = mod[1, :b].reshape(b, 6, d)

    cos, sup, sdn = _rope_tables(s)
    lane = jnp.arange(LANES)
    bd = (lane[:, None] // HEAD_DIM == lane[None, :] // HEAD_DIM).astype(BF16)
    qg2 = jnp.tile(q_gain[0], LANES // HEAD_DIM)[None, :]
    kg2 = jnp.tile(k_gain[0], LANES // HEAD_DIM)[None, :]
    w_in_b = w_in[0].astype(BF16)
    bsp = jnp.broadcast_to(b_sp[0][:, :, None], (N_SG_GROUPS, CHUNK, LANES))

    q, k, v, sg = _l0_pre(x, mod0, g_mix[0:1], w_in_b, qg2, kg2, cos, sup, sdn, bd, w_sp[0].astype(BF16), bsp)
    kc, vc = _ctx_kv(ctx, cmod0, g_mix[0:1], w_in_b[:, aw:aw + 2 * kvw], kg2, bd)
    attn = _attention(q, k, kc, v, vc)

    assert w_ffn_out.shape[1] % FFN_CHUNK == 0 and d % FFN_CHUNK == 0, (w_ffn_out.shape, d)
    wi_b = w_ffn_in.astype(BF16)
    wd_b = w_ffn_out.astype(BF16)
    h, u = _l0_post(x, attn, sg, mod0, w_out[0].astype(BF16), g_ffn[0:1], wi_b, wd_b,
                    mod1, g_mix[1:2], w_pw1[0].astype(BF16), b_pw1[0:1])
    w_dw_s = w_dw[0].reshape(CONV_WIDTH, d // LANES, LANES).transpose(1, 0, 2)
    b_dw_s = b_dw[0].reshape(d // LANES, 1, LANES)
    return _l1_post(h, u, mod1, w_dw_s, b_dw_s, ln_g[0:1], ln_b[0:1], w_pw2[0].astype(BF16), b_pw2[0:1],
                    g_ffn[1:2], wi_b, wd_b, g_final[None, :])
```

```python
import functools

import jax
import jax.numpy as jnp
import numpy as np
from jax import lax
from jax.experimental import pallas as pl
from jax.experimental.pallas import tpu as pltpu

F32 = jnp.float32
BF16 = jnp.bfloat16

HEAD_DIM = 64
N_Q_HEADS = 8
N_KV_HEADS = 2
GRID_W = 64
CHUNK = 128
N_SG_GROUPS = 4
ROPE_THETA = 10000.0
CONV_WIDTH = 31
EPS = 1e-6
Q_SCALE = HEAD_DIM ** -0.5 * 1.4426950408889634

LANES = 128
HALO = 16
FFN_CHUNK = 256
VMEM_LIMIT = 56 * 1024 * 1024

TM = 512
TQ = 512
TK = 256


def _const_spec(shape):
    nd = len(shape)
    return pl.BlockSpec(shape, lambda *_: (0,) * nd, pipeline_mode=pl.Buffered(1))


def _params(n_axes, flags=None):
    return pltpu.CompilerParams(dimension_semantics=("arbitrary",) * n_axes,
                                vmem_limit_bytes=VMEM_LIMIT, flags=flags)


def _dot(a, b):
    return jnp.dot(a, b, preferred_element_type=F32)


def _sigmoid(x):
    return 1.0 / (1.0 + jnp.exp(-x))


def _gelu_tanh(x):
    c = 0.7978845608028654
    return x * (0.5 * (1.0 + jnp.tanh(c * (x + 0.044715 * (x * x * x)))))


def _rms_mod(x, g, sc, sh):
    ms = jnp.mean(x * x, axis=-1, keepdims=True)
    return (x * lax.rsqrt(ms + EPS) * g) * (1.0 + sc) + sh


def _head_rms(z, bd, gain):
    z2 = z * z
    hi = z2.astype(BF16)
    lo = (z2 - hi.astype(F32)).astype(BF16)
    ssq = _dot(hi, bd) + _dot(lo, bd)
    return z * lax.rsqrt(ssq * (1.0 / HEAD_DIM) + EPS) * gain


def _rope(z, cos, s_up, s_dn):
    return z * cos + pltpu.roll(z, LANES - 16, 1) * s_up + pltpu.roll(z, 16, 1) * s_dn


def _adaln_kernel(c_ref, w_ref, b_ref, o_ref):
    c = c_ref[...]
    s = (c * _sigmoid(c)).astype(BF16)
    o_ref[...] = _dot(s, w_ref[...].astype(BF16)) + b_ref[...]


def _adaln(cvec, w_mod, b_mod):
    depth, d, n = w_mod.shape
    rows = cvec.shape[0]
    tn = 1536
    return pl.pallas_call(
        _adaln_kernel,
        grid=(depth, n // tn),
        in_specs=[
            pl.BlockSpec((rows, d), lambda l, j: (0, 0)),
            pl.BlockSpec((None, d, tn), lambda l, j: (l, 0, j)),
            pl.BlockSpec((None, 1, tn), lambda l, j: (l, 0, j)),
        ],
        out_specs=pl.BlockSpec((None, rows, tn), lambda l, j: (l, 0, j)),
        out_shape=jax.ShapeDtypeStruct((depth, rows, n), F32),
        compiler_params=_params(2),
        name="adaln",
    )(cvec, w_mod, b_mod.reshape(depth, 1, n))


def _l0_pre_kernel(x_ref, mod_ref, g_ref, w_ref, qg_ref, kg_ref, cos_ref, sup_ref, sdn_ref, bd_ref,
                   wsp_ref, bsp_ref, q_ref, k_ref, v_ref, sg_ref, pa_scr, pb_scr):
    i = pl.program_id(1)
    aw = N_Q_HEADS * HEAD_DIM
    kvw = N_KV_HEADS * HEAD_DIM
    sgw = N_SG_GROUPS * LANES
    o_u = aw + 2 * kvw
    o_v = o_u + sgw
    tm = x_ref.shape[0]
    nchunk = tm // CHUNK

    @pl.when(i == 0)
    def _zero():
        pb_scr[...] = jnp.zeros(pb_scr.shape, F32)

    def step(p_new, p_old):
        xm = _rms_mod(x_ref[...], g_ref[...], mod_ref[1:2, :], mod_ref[0:1, :]).astype(BF16)
        p_new[...] = _dot(xm, w_ref[...])
        bd = bd_ref[...]
        cos, sup, sdn = cos_ref[...], sup_ref[...], sdn_ref[...]
        for j in range(aw // LANES):
            qn = _head_rms(p_old[:, j * LANES:(j + 1) * LANES], bd, qg_ref[...])
            q_ref[j] = (_rope(qn, cos, sup, sdn) * Q_SCALE).T.astype(BF16)
        kn = _head_rms(p_old[:, aw:aw + kvw], bd, kg_ref[...])
        k_ref[...] = _rope(kn, cos, sup, sdn).astype(BF16)
        v_ref[...] = p_old[:, aw + kvw:aw + 2 * kvw].T.astype(BF16)
        for g in range(N_SG_GROUPS):
            gv = _gelu_tanh(p_old[:, o_v + g * LANES:o_v + (g + 1) * LANES])
            mu = jnp.mean(gv, axis=-1, keepdims=True)
            dv = gv - mu
            var = jnp.mean(dv * dv, axis=-1, keepdims=True)
            vn = (dv * lax.rsqrt(var + EPS)).astype(BF16)
            rhs = jnp.concatenate([vn[c * CHUNK:(c + 1) * CHUNK, :] for c in range(nchunk)], axis=1)
            mixed = _dot(wsp_ref[g], rhs)
            gu = _gelu_tanh(p_old[:, o_u + g * LANES:o_u + (g + 1) * LANES])
            for c in range(nchunk):
                blk = mixed[:, c * CHUNK:(c + 1) * CHUNK] + bsp_ref[g]
                sg_ref[c * CHUNK:(c + 1) * CHUNK, g * LANES:(g + 1) * LANES] = (
                    gu[c * CHUNK:(c + 1) * CHUNK, :] * blk).astype(BF16)

    @pl.when(lax.rem(i, 2) == 0)
    def _even():
        step(pa_scr, pb_scr)

    @pl.when(lax.rem(i, 2) == 1)
    def _odd():
        step(pb_scr, pa_scr)


def _l0_pre(x, mod, g_mix, w_in, qg2, kg2, cos, sup, sdn, bd, w_sp, bsp):
    b, s, d = x.shape
    inw = w_in.shape[1]
    npair = N_Q_HEADS * HEAD_DIM // LANES
    kvw = N_KV_HEADS * HEAD_DIM
    sgw = N_SG_GROUPS * LANES
    n_i = s // TM
    prev = lambda i: jnp.maximum(i - 1, 0)
    row = lambda bi, i: (bi, prev(i), 0)
    tab = pl.BlockSpec((TM, LANES), lambda bi, i: (prev(i), 0))
    return pl.pallas_call(
        _l0_pre_kernel,
        grid=(b, n_i + 1),
        in_specs=[
            pl.BlockSpec((None, TM, d), lambda bi, i: (bi, jnp.minimum(i, n_i - 1), 0)),
            pl.BlockSpec((None, 6, d), lambda bi, i: (bi, 0, 0)),
            _const_spec((1, d)),
            _const_spec((d, inw)),
            _const_spec((1, LANES)),
            _const_spec((1, LANES)),
            tab, tab, tab,
            _const_spec((LANES, LANES)),
            _const_spec((N_SG_GROUPS, CHUNK, CHUNK)),
            _const_spec((N_SG_GROUPS, CHUNK, LANES)),
        ],
        out_specs=[
            pl.BlockSpec((None, npair, LANES, TM), lambda bi, i: (bi, 0, 0, prev(i))),
            pl.BlockSpec((None, TM, kvw), row),
            pl.BlockSpec((None, kvw, TM), lambda bi, i: (bi, 0, prev(i))),
            pl.BlockSpec((None, TM, sgw), row),
        ],
        out_shape=[
            jax.ShapeDtypeStruct((b, npair, LANES, s), BF16),
            jax.ShapeDtypeStruct((b, s, kvw), BF16),
            jax.ShapeDtypeStruct((b, kvw, s), BF16),
            jax.ShapeDtypeStruct((b, s, sgw), BF16),
        ],
        scratch_shapes=[pltpu.VMEM((TM, inw), F32), pltpu.VMEM((TM, inw), F32)],
        compiler_params=_params(2),
        name="l0_pre",
    )(x, mod, g_mix, w_in, qg2, kg2, cos, sup, sdn, bd, w_sp, bsp)


def _ctx_kv_kernel(x_ref, mod_ref, g_ref, w_ref, kg_ref, bd_ref, k_ref, v_ref):
    kvw = N_KV_HEADS * HEAD_DIM
    xm = _rms_mod(x_ref[...], g_ref[...], mod_ref[1:2, :], mod_ref[0:1, :]).astype(BF16)
    p = _dot(xm, w_ref[...])
    k_ref[...] = _head_rms(p[:, 0:kvw], bd_ref[...], kg_ref[...]).astype(BF16)
    v_ref[...] = p[:, kvw:2 * kvw].T.astype(BF16)


def _ctx_kv(ctx, cmod, g_mix, w_kv, kg2, bd):
    b, m, d = ctx.shape
    kvw = N_KV_HEADS * HEAD_DIM
    row = lambda bi: (bi, 0, 0)
    return pl.pallas_call(
        _ctx_kv_kernel,
        grid=(b,),
        in_specs=[
            pl.BlockSpec((None, m, d), row),
            _const_spec((6, d)),
            _const_spec((1, d)),
            _const_spec((d, 2 * kvw)),
            _const_spec((1, LANES)),
            _const_spec((LANES, LANES)),
        ],
        out_specs=[pl.BlockSpec((None, m, kvw), row), pl.BlockSpec((None, kvw, m), row)],
        out_shape=[jax.ShapeDtypeStruct((b, m, kvw), BF16), jax.ShapeDtypeStruct((b, kvw, m), BF16)],
        compiler_params=_params(1),
        name="ctx_kv",
    )(ctx, cmod, g_mix, w_kv, kg2, bd)


NEG_INIT = -1e30
VX_ROWS = HEAD_DIM + 16
Q_HALF = 256
S_SLOTS = 2


def _attn_kernel(q_ref, k_ref, kc_ref, v_ref, vc_ref, o_ref, ka_scr, kb_scr, vx_scr, s_scr, mx_scr, m_scr, acc_scr):
    h = pl.program_id(1)
    i = pl.program_id(2)
    tq = q_ref.shape[2]
    n_ctx = kc_ref.shape[0]
    n_keys = ka_scr.shape[0]
    chunks = [(r0, min(TK, n_ctx - r0)) for r0 in range(0, n_ctx, TK)]
    chunks += [(r0, TK) for r0 in range(n_ctx, n_keys, TK)]

    @pl.when(i == 0)
    def _fill():
        hrow = pl.ds(pl.multiple_of(h * HEAD_DIM, HEAD_DIM), HEAD_DIM)

        def put_k(kk, r0):
            lo = lax.broadcasted_iota(jnp.int32, kk.shape, 1) < HEAD_DIM
            kk = kk.astype(F32)
            ksw = pltpu.roll(kk, HEAD_DIM, 1)
            k_lo = jnp.where(h == 0, kk, ksw)
            k_hi = jnp.where(h == 0, ksw, kk)
            ka_scr[r0:r0 + kk.shape[0], :] = jnp.where(lo, k_lo, 0.0).astype(BF16)
            kb_scr[r0:r0 + kk.shape[0], :] = jnp.where(lo, 0.0, k_hi).astype(BF16)

        for r0, nk in chunks:
            if r0 < n_ctx:
                put_k(kc_ref[r0:r0 + nk, :], r0)
                vx_scr[0:HEAD_DIM, r0:r0 + nk] = vc_ref[hrow, r0:r0 + nk]
            else:
                put_k(k_ref[r0 - n_ctx:r0 - n_ctx + nk, :], r0)
                vx_scr[0:HEAD_DIM, r0:r0 + nk] = v_ref[hrow, r0 - n_ctx:r0 - n_ctx + nk]
            ones_row = lax.broadcasted_iota(jnp.int32, (VX_ROWS - HEAD_DIM, nk), 0) == 0
            vx_scr[HEAD_DIM:, r0:r0 + nk] = jnp.where(ones_row, 1.0, 0.0).astype(BF16)

    m_scr[...] = jnp.full(m_scr.shape, NEG_INIT, F32)
    acc_scr[...] = jnp.zeros(acc_scr.shape, F32)
    heads = range(m_scr.shape[0])

    def scores(c, g):
        r0, nk = chunks[c]
        k_scr = kb_scr if g % 2 else ka_scr
        s = _dot(k_scr[r0:r0 + nk, :], q_ref[g // 2])
        s_scr[c % S_SLOTS, g, 0:nk, :] = s
        mx_scr[c % S_SLOTS, g] = jnp.max(s, axis=0, keepdims=True)

    def softmax_pv(c, g):
        r0, nk = chunks[c]
        m_old = m_scr[g]
        m_new = jnp.maximum(m_old, mx_scr[c % S_SLOTS, g])
        alpha = jnp.exp2(m_old - m_new)
        for q0 in range(0, tq, Q_HALF):
            qs = slice(q0, q0 + Q_HALF)
            p = jnp.exp2(s_scr[c % S_SLOTS, g, 0:nk, qs] - m_new[:, qs]).astype(BF16)
            acc_scr[g, :, qs] = alpha[:, qs] * acc_scr[g, :, qs] + _dot(vx_scr[:, r0:r0 + nk], p)
        m_scr[g] = m_new

    ahead = S_SLOTS - 1
    for c in range(min(ahead, len(chunks))):
        for g in heads:
            scores(c, g)
    for c in range(len(chunks)):
        for g in heads:
            softmax_pv(c, g)
            if c + ahead < len(chunks):
                scores(c + ahead, g)

    for pair in range(len(heads) // 2):
        a = acc_scr[2 * pair]
        b = acc_scr[2 * pair + 1]
        st = jnp.concatenate([a[0:HEAD_DIM] / a[HEAD_DIM:HEAD_DIM + 1], b[0:HEAD_DIM] / b[HEAD_DIM:HEAD_DIM + 1]],
                             axis=0)
        o_ref[:, pair * LANES:(pair + 1) * LANES] = st.T.astype(BF16)


def _attention(q, k, kc, v, vc):
    b, npair, _, s = q.shape
    m = kc.shape[1]
    kvw = k.shape[2]
    pairs_per_kv = npair // N_KV_HEADS
    gw = pairs_per_kv * LANES
    assert s % TK == 0 and m % LANES == 0, (m, s, TK)
    full = lambda bi, h, i: (bi, 0, 0)
    return pl.pallas_call(
        _attn_kernel,
        grid=(b, N_KV_HEADS, s // TQ),
        in_specs=[
            pl.BlockSpec((None, pairs_per_kv, LANES, TQ), lambda bi, h, i: (bi, h, 0, i)),
            pl.BlockSpec((None, s, kvw), full),
            pl.BlockSpec((None, m, kvw), full),
            pl.BlockSpec((None, kvw, s), full),
            pl.BlockSpec((None, kvw, m), full),
        ],
        out_specs=pl.BlockSpec((None, TQ, gw), lambda bi, h, i: (bi, i, h)),
        out_shape=jax.ShapeDtypeStruct((b, s, N_KV_HEADS * gw), BF16),
        scratch_shapes=[
            pltpu.VMEM((m + s, LANES), BF16),
            pltpu.VMEM((m + s, LANES), BF16),
            pltpu.VMEM((VX_ROWS, m + s), BF16),
            pltpu.VMEM((S_SLOTS, 2 * pairs_per_kv, TK, TQ), F32),
            pltpu.VMEM((S_SLOTS, 2 * pairs_per_kv, 1, TQ), F32),
            pltpu.VMEM((2 * pairs_per_kv, 1, TQ), F32),
            pltpu.VMEM((2 * pairs_per_kv, VX_ROWS, TQ), F32),
        ],
        compiler_params=_params(3),
        name="attn",
    )(q, k, kc, v, vc)


def _ffn_residual(h, mod_ref, g_ref, wi_ref, wo_ref, xm_scr, acc_scr):
    dff = wo_ref.shape[0]
    xm_scr[...] = _rms_mod(h, g_ref[...], mod_ref[4:5, :], mod_ref[3:4, :]).astype(BF16)
    acc_scr[...] = jnp.zeros(acc_scr.shape, F32)

    for c0 in range(0, dff, FFN_CHUNK):
        xm = xm_scr[...]
        g = _dot(xm, wi_ref[:, c0:c0 + FFN_CHUNK])
        u = _dot(xm, wi_ref[:, dff + c0:dff + c0 + FFN_CHUNK])
        a = (g * _sigmoid(g) * u).astype(BF16)
        acc_scr[...] += _dot(a, wo_ref[c0:c0 + FFN_CHUNK, :])
    return h + mod_ref[5:6, :] * acc_scr[...]


def _layer_spec(w, layer):
    return pl.BlockSpec((None,) + w.shape[1:], lambda *_: (layer, 0, 0), pipeline_mode=pl.Buffered(1))


def _ffn_specs(d, wi, wd, layer):
    return [_const_spec((1, d)), _layer_spec(wi, layer), _layer_spec(wd, layer)]


def _ffn_scratch(d):
    return [pltpu.VMEM((TM, d), BF16), pltpu.VMEM((TM, d), F32)]


def _l0_post_kernel(x_ref, a_ref, sg_ref, mod_ref, wo_ref, gf_ref, wi_ref, wd_ref, mod1_ref, g1_ref, w1_ref, b1_ref,
                    h_ref, u_ref, xm_scr, acc_scr):
    aw = a_ref.shape[1]
    d = x_ref.shape[1]
    y = _dot(a_ref[...], wo_ref[0:aw, :]) + _dot(sg_ref[...], wo_ref[aw:, :])
    h = x_ref[...] + mod_ref[2:3, :] * y
    h = _ffn_residual(h, mod_ref, gf_ref, wi_ref, wd_ref, xm_scr, acc_scr)
    h_ref[...] = h
    xm_scr[...] = _rms_mod(h, g1_ref[...], mod1_ref[1:2, :], mod1_ref[0:1, :]).astype(BF16)
    for c0 in range(0, d, FFN_CHUNK):
        xm = xm_scr[...]
        a = _dot(xm, w1_ref[:, c0:c0 + FFN_CHUNK]) + b1_ref[:, c0:c0 + FFN_CHUNK]
        gate = _dot(xm, w1_ref[:, d + c0:d + c0 + FFN_CHUNK]) + b1_ref[:, d + c0:d + c0 + FFN_CHUNK]
        u_ref[:, c0:c0 + FFN_CHUNK] = a * _sigmoid(gate)


def _l0_post(x, attn, sg, mod, w_out, g_ffn, wi, wd, mod1, g_mix1, w_pw1, b_pw1):
    b, s, d = x.shape
    row = lambda bi, i: (bi, i, 0)
    mods = pl.BlockSpec((None, 6, d), lambda bi, i: (bi, 0, 0))
    tile = pl.BlockSpec((None, TM, d), row)
    return pl.pallas_call(
        _l0_post_kernel,
        grid=(b, s // TM),
        in_specs=[
            tile,
            pl.BlockSpec((None, TM, attn.shape[2]), row),
            pl.BlockSpec((None, TM, sg.shape[2]), row),
            mods,
            _const_spec(w_out.shape),
        ] + _ffn_specs(d, wi, wd, 0) + [mods, _const_spec((1, d)), _const_spec(w_pw1.shape), _const_spec((1, 2 * d))],
        out_specs=[tile, tile],
        out_shape=[jax.ShapeDtypeStruct((b, s, d), F32), jax.ShapeDtypeStruct((b, s, d), F32)],
        scratch_shapes=_ffn_scratch(d),
        compiler_params=_params(2),
        name="l0_post",
    )(x, attn, sg, mod, w_out, g_ffn, wi, wd, mod1, g_mix1, w_pw1, b_pw1)


def _l1_post_kernel(h_ref, u_ref, up_ref, un_ref, mod_ref, wdw_ref, bdw_ref, lng_ref, lnb_ref, w2_ref, b2_ref,
                    gf_ref, wi_ref, wd_ref, gfin_ref, o_ref, e_scr, c_scr, xm_scr, acc_scr):
    i = pl.program_id(1)
    n_i = pl.num_programs(1)
    tm = h_ref.shape[0]
    nslab = e_scr.shape[0]
    up = jnp.where(i > 0, up_ref[...], 0.0)
    un = jnp.where(i < n_i - 1, un_ref[...], 0.0)
    for sl in range(nslab):
        cols = slice(sl * LANES, (sl + 1) * LANES)
        e_scr[sl, 0:HALO, :] = up[:, cols]
        e_scr[sl, HALO:HALO + tm, :] = u_ref[:, cols]
        e_scr[sl, HALO + tm:, :] = un[:, cols]
    rb = 64
    off = HALO - CONV_WIDTH // 2

    def conv_slab(sl, carry):
        for r0 in range(0, tm, rb):
            acc = jnp.broadcast_to(bdw_ref[sl], (rb, LANES))
            for j in range(CONV_WIDTH):
                acc = acc + wdw_ref[sl, j:j + 1, :] * e_scr[sl, r0 + off + j:r0 + off + j + rb, :]
            c_scr[sl, r0:r0 + rb, :] = acc
        return carry

    lax.fori_loop(0, nslab, conv_slab, 0)
    cv = jnp.concatenate([c_scr[sl] for sl in range(nslab)], axis=1)
    mu = jnp.mean(cv, axis=-1, keepdims=True)
    dv = cv - mu
    var = jnp.mean(dv * dv, axis=-1, keepdims=True)
    ln = dv * lax.rsqrt(var + EPS) * lng_ref[...] + lnb_ref[...]
    act = (ln * _sigmoid(ln)).astype(BF16)
    y = _dot(act, w2_ref[...]) + b2_ref[...]
    h = h_ref[...] + mod_ref[2:3, :] * y
    h = _ffn_residual(h, mod_ref, gf_ref, wi_ref, wd_ref, xm_scr, acc_scr)
    ms = jnp.mean(h * h, axis=-1, keepdims=True)
    o_ref[...] = h * lax.rsqrt(ms + EPS) * gfin_ref[...]


def _l1_post(h, u, mod, w_dw, b_dw, ln_g, ln_b, w_pw2, b_pw2, g_ffn, wi, wd, g_final):
    b, s, d = h.shape
    row = lambda bi, i: (bi, i, 0)
    per = TM // HALO
    last = s // HALO - 1
    vec = _const_spec((1, d))
    return pl.pallas_call(
        _l1_post_kernel,
        grid=(b, s // TM),
        in_specs=[
            pl.BlockSpec((None, TM, d), row),
            pl.BlockSpec((None, TM, d), row),
            pl.BlockSpec((None, HALO, d), lambda bi, i: (bi, jnp.maximum(i * per - 1, 0), 0)),
            pl.BlockSpec((None, HALO, d), lambda bi, i: (bi, jnp.minimum((i + 1) * per, last), 0)),
            pl.BlockSpec((None, 6, d), lambda bi, i: (bi, 0, 0)),
            _const_spec(w_dw.shape), _const_spec(b_dw.shape), vec, vec,
            _const_spec(w_pw2.shape), vec,
        ] + _ffn_specs(d, wi, wd, 1) + [vec],
        out_specs=pl.BlockSpec((None, TM, d), row),
        out_shape=jax.ShapeDtypeStruct((b, s, d), F32),
        scratch_shapes=[pltpu.VMEM((d // LANES, TM + 2 * HALO, LANES), F32),
                        pltpu.VMEM((d // LANES, TM, LANES), F32)] + _ffn_scratch(d),
        compiler_params=_params(2),
        name="l1_post",
    )(h, u, u, u, mod, w_dw, b_dw, ln_g, ln_b, w_pw2, b_pw2, g_ffn, wi, wd, g_final)


def _rope_tables(n):
    half = HEAD_DIM // 2
    pos = np.arange(n)
    row = (pos // GRID_W).astype(np.float32)
    col = (pos % GRID_W).astype(np.float32)
    inv = (ROPE_THETA ** (-np.arange(0, half, 2, dtype=np.float32) / half)).astype(np.float32)
    lane = np.arange(LANES)
    hd = lane % HEAD_DIM
    w = hd % half
    ang = np.where((hd < half)[None, :], row[:, None], col[:, None]) * inv[w % (half // 2)][None, :]
    first = (w < half // 2)[None, :]
    sin = np.sin(ang)
    tabs = (np.cos(ang), np.where(first, -sin, 0.0), np.where(first, 0.0, sin))
    return tuple(jnp.asarray(t, F32) for t in tabs)


def kernel(x, c, ctx, c_ctx, w_mod, b_mod, g_mix, g_ffn, w_ffn_in, w_ffn_out, w_in, q_gain, k_gain, w_sp, b_sp,
           w_out, w_pw1, b_pw1, w_dw, b_dw, ln_g, ln_b, w_pw2, b_pw2, g_final):
    b, s, d = x.shape
    aw = N_Q_HEADS * HEAD_DIM
    kvw = N_KV_HEADS * HEAD_DIM

    rows = -(-(b + 1) // 8) * 8
    cvec = jnp.concatenate([c, c_ctx[None, :], jnp.zeros((rows - b - 1, d), F32)], axis=0)
    mod = _adaln(cvec, w_mod, b_mod)
    mod0 = mod[0, :b].reshape(b, 6, d)
    cmod0 = mod[0, b].reshape(6, d)
    mod1 = mod[1, :b].reshape(b, 6, d)

    cos, sup, sdn = _rope_tables(s)
    lane = jnp.arange(LANES)
    bd = (lane[:, None] // HEAD_DIM == lane[None, :] // HEAD_DIM).astype(BF16)
    qg2 = jnp.tile(q_gain[0], LANES // HEAD_DIM)[None, :]
    kg2 = jnp.tile(k_gain[0], LANES // HEAD_DIM)[None, :]
    w_in_b = w_in[0].astype(BF16)
    bsp = jnp.broadcast_to(b_sp[0][:, :, None], (N_SG_GROUPS, CHUNK, LANES))

    q, k, v, sg = _l0_pre(x, mod0, g_mix[0:1], w_in_b, qg2, kg2, cos, sup, sdn, bd, w_sp[0].astype(BF16), bsp)
    kc, vc = _ctx_kv(ctx, cmod0, g_mix[0:1], w_in_b[:, aw:aw + 2 * kvw], kg2, bd)
    attn = _attention(q, k, kc, v, vc)

    assert w_ffn_out.shape[1] % FFN_CHUNK == 0 and d % FFN_CHUNK == 0, (w_ffn_out.shape, d)
    wi_b = w_ffn_in.astype(BF16)
    wd_b = w_ffn_out.astype(BF16)
    h, u = _l0_post(x, attn, sg, mod0, w_out[0].astype(BF16), g_ffn[0:1], wi_b, wd_b,
                    mod1, g_mix[1:2], w_pw1[0].astype(BF16), b_pw1[0:1])
    w_dw_s = w_dw[0].reshape(CONV_WIDTH, d // LANES, LANES).transpose(1, 0, 2)
    b_dw_s = b_dw[0].reshape(d // LANES, 1, LANES)
    return _l1_post(h, u, mod1, w_dw_s, b_dw_s, ln_g[0:1], ln_b[0:1], w_pw2[0].astype(BF16), b_pw2[0:1],
                    g_ffn[1:2], wi_b, wd_b, g_final[None, :])
```

```python
import functools

import jax
import jax.numpy as jnp
import numpy as np
from jax import lax
from jax.experimental import pallas as pl
from jax.experimental.pallas import tpu as pltpu

F32 = jnp.float32
BF16 = jnp.bfloat16

HEAD_DIM = 64
N_Q_HEADS = 8
N_KV_HEADS = 2
GRID_W = 64
CHUNK = 128
N_SG_GROUPS = 4
ROPE_THETA = 10000.0
CONV_WIDTH = 31
EPS = 1e-6
Q_SCALE = HEAD_DIM ** -0.5 * 1.4426950408889634

LANES = 128
HALO = 16
FFN_CHUNK = 256
VMEM_LIMIT = 56 * 1024 * 1024

TM = 512
TQ = 512
TK = 256


def _const_spec(shape):
    nd = len(shape)
    return pl.BlockSpec(shape, lambda *_: (0,) * nd, pipeline_mode=pl.Buffered(1))


def _params(n_axes, flags=None):
    return pltpu.CompilerParams(dimension_semantics=("arbitrary",) * n_axes,
                                vmem_limit_bytes=VMEM_LIMIT, flags=flags)


def _dot(a, b):
    return jnp.dot(a, b, preferred_element_type=F32)


def _sigmoid(x):
    return 1.0 / (1.0 + jnp.exp(-x))


def _gelu_tanh(x):
    c = 0.7978845608028654
    return x * (0.5 * (1.0 + jnp.tanh(c * (x + 0.044715 * (x * x * x)))))


def _rms_mod(x, g, sc, sh):
    ms = jnp.mean(x * x, axis=-1, keepdims=True)
    return (x * lax.rsqrt(ms + EPS) * g) * (1.0 + sc) + sh


def _head_rms(z, bd, gain):
    ssq = _dot((z * z).astype(BF16), bd)
    return z * lax.rsqrt(ssq * (1.0 / HEAD_DIM) + EPS) * gain


def _rope(z, cos, s_up, s_dn):
    return z * cos + pltpu.roll(z, LANES - 16, 1) * s_up + pltpu.roll(z, 16, 1) * s_dn


def _adaln_kernel(c_ref, w_ref, b_ref, o_ref):
    c = c_ref[...]
    s = (c * _sigmoid(c)).astype(BF16)
    o_ref[...] = _dot(s, w_ref[...].astype(BF16)) + b_ref[...]


def _adaln(cvec, w_mod, b_mod):
    depth, d, n = w_mod.shape
    rows = cvec.shape[0]
    tn = 1536
    return pl.pallas_call(
        _adaln_kernel,
        grid=(depth, n // tn),
        in_specs=[
            pl.BlockSpec((rows, d), lambda l, j: (0, 0)),
            pl.BlockSpec((None, d, tn), lambda l, j: (l, 0, j)),
            pl.BlockSpec((None, 1, tn), lambda l, j: (l, 0, j)),
        ],
        out_specs=pl.BlockSpec((None, rows, tn), lambda l, j: (l, 0, j)),
        out_shape=jax.ShapeDtypeStruct((depth, rows, n), F32),
        compiler_params=_params(2),
        name="adaln",
    )(cvec, w_mod, b_mod.reshape(depth, 1, n))


def _l0_pre_kernel(x_ref, mod_ref, g_ref, w_ref, qg_ref, kg_ref, cos_ref, sup_ref, sdn_ref, bd_ref,
                   wsp_ref, bsp_ref, q_ref, k_ref, v_ref, sg_ref, pa_scr, pb_scr):
    i = pl.program_id(1)
    aw = N_Q_HEADS * HEAD_DIM
    kvw = N_KV_HEADS * HEAD_DIM
    sgw = N_SG_GROUPS * LANES
    o_u = aw + 2 * kvw
    o_v = o_u + sgw
    tm = x_ref.shape[0]
    nchunk = tm // CHUNK

    @pl.when(i == 0)
    def _zero():
        pb_scr[...] = jnp.zeros(pb_scr.shape, F32)

    def step(p_new, p_old):
        xm = _rms_mod(x_ref[...], g_ref[...], mod_ref[1:2, :], mod_ref[0:1, :]).astype(BF16)
        p_new[...] = _dot(xm, w_ref[...])
        bd = bd_ref[...]
        cos, sup, sdn = cos_ref[...], sup_ref[...], sdn_ref[...]
        for j in range(aw // LANES):
            qn = _head_rms(p_old[:, j * LANES:(j + 1) * LANES], bd, qg_ref[...])
            q_ref[j] = (_rope(qn, cos, sup, sdn) * Q_SCALE).T.astype(BF16)
        kn = _head_rms(p_old[:, aw:aw + kvw], bd, kg_ref[...])
        k_ref[...] = _rope(kn, cos, sup, sdn).astype(BF16)
        v_ref[...] = p_old[:, aw + kvw:aw + 2 * kvw].T.astype(BF16)
        for g in range(N_SG_GROUPS):
            gv = _gelu_tanh(p_old[:, o_v + g * LANES:o_v + (g + 1) * LANES])
            mu = jnp.mean(gv, axis=-1, keepdims=True)
            dv = gv - mu
            var = jnp.mean(dv * dv, axis=-1, keepdims=True)
            vn = (dv * lax.rsqrt(var + EPS)).astype(BF16)
            rhs = jnp.concatenate([vn[c * CHUNK:(c + 1) * CHUNK, :] for c in range(nchunk)], axis=1)
            mixed = _dot(wsp_ref[g], rhs)
            gu = _gelu_tanh(p_old[:, o_u + g * LANES:o_u + (g + 1) * LANES])
            for c in range(nchunk):
                blk = mixed[:, c * CHUNK:(c + 1) * CHUNK] + bsp_ref[g]
                sg_ref[c * CHUNK:(c + 1) * CHUNK, g * LANES:(g + 1) * LANES] = (
                    gu[c * CHUNK:(c + 1) * CHUNK, :] * blk).astype(BF16)

    @pl.when(lax.rem(i, 2) == 0)
    def _even():
        step(pa_scr, pb_scr)

    @pl.when(lax.rem(i, 2) == 1)
    def _odd():
        step(pb_scr, pa_scr)


def _l0_pre(x, mod, g_mix, w_in, qg2, kg2, cos, sup, sdn, bd, w_sp, bsp):
    b, s, d = x.shape
    inw = w_in.shape[1]
    npair = N_Q_HEADS * HEAD_DIM // LANES
    kvw = N_KV_HEADS * HEAD_DIM
    sgw = N_SG_GROUPS * LANES
    n_i = s // TM
    prev = lambda i: jnp.maximum(i - 1, 0)
    row = lambda bi, i: (bi, prev(i), 0)
    tab = pl.BlockSpec((TM, LANES), lambda bi, i: (prev(i), 0))
    return pl.pallas_call(
        _l0_pre_kernel,
        grid=(b, n_i + 1),
        in_specs=[
            pl.BlockSpec((None, TM, d), lambda bi, i: (bi, jnp.minimum(i, n_i - 1), 0)),
            pl.BlockSpec((None, 6, d), lambda bi, i: (bi, 0, 0)),
            _const_spec((1, d)),
            _const_spec((d, inw)),
            _const_spec((1, LANES)),
            _const_spec((1, LANES)),
            tab, tab, tab,
            _const_spec((LANES, LANES)),
            _const_spec((N_SG_GROUPS, CHUNK, CHUNK)),
            _const_spec((N_SG_GROUPS, CHUNK, LANES)),
        ],
        out_specs=[
            pl.BlockSpec((None, npair, LANES, TM), lambda bi, i: (bi, 0, 0, prev(i))),
            pl.BlockSpec((None, TM, kvw), row),
            pl.BlockSpec((None, kvw, TM), lambda bi, i: (bi, 0, prev(i))),
            pl.BlockSpec((None, TM, sgw), row),
        ],
        out_shape=[
            jax.ShapeDtypeStruct((b, npair, LANES, s), BF16),
            jax.ShapeDtypeStruct((b, s, kvw), BF16),
            jax.ShapeDtypeStruct((b, kvw, s), BF16),
            jax.ShapeDtypeStruct((b, s, sgw), BF16),
        ],
        scratch_shapes=[pltpu.VMEM((TM, inw), F32), pltpu.VMEM((TM, inw), F32)],
        compiler_params=_params(2),
        name="l0_pre",
    )(x, mod, g_mix, w_in, qg2, kg2, cos, sup, sdn, bd, w_sp, bsp)


def _ctx_kv_kernel(x_ref, mod_ref, g_ref, w_ref, kg_ref, bd_ref, k_ref, v_ref):
    kvw = N_KV_HEADS * HEAD_DIM
    xm = _rms_mod(x_ref[...], g_ref[...], mod_ref[1:2, :], mod_ref[0:1, :]).astype(BF16)
    p = _dot(xm, w_ref[...])
    k_ref[...] = _head_rms(p[:, 0:kvw], bd_ref[...], kg_ref[...]).astype(BF16)
    v_ref[...] = p[:, kvw:2 * kvw].T.astype(BF16)


def _ctx_kv(ctx, cmod, g_mix, w_kv, kg2, bd):
    b, m, d = ctx.shape
    kvw = N_KV_HEADS * HEAD_DIM
    row = lambda bi: (bi, 0, 0)
    return pl.pallas_call(
        _ctx_kv_kernel,
        grid=(b,),
        in_specs=[
            pl.BlockSpec((None, m, d), row),
            _const_spec((6, d)),
            _const_spec((1, d)),
            _const_spec((d, 2 * kvw)),
            _const_spec((1, LANES)),
            _const_spec((LANES, LANES)),
        ],
        out_specs=[pl.BlockSpec((None, m, kvw), row), pl.BlockSpec((None, kvw, m), row)],
        out_shape=[jax.ShapeDtypeStruct((b, m, kvw), BF16), jax.ShapeDtypeStruct((b, kvw, m), BF16)],
        compiler_params=_params(1),
        name="ctx_kv",
    )(ctx, cmod, g_mix, w_kv, kg2, bd)


NEG_INIT = -1e30
VX_ROWS = HEAD_DIM + 16
Q_HALF = 256
S_SLOTS = 2


def _attn_kernel(q_ref, k_ref, kc_ref, v_ref, vc_ref, o_ref, ka_scr, kb_scr, vx_scr, s_scr, mx_scr, m_scr, acc_scr):
    h = pl.program_id(1)
    i = pl.program_id(2)
    tq = q_ref.shape[2]
    n_ctx = kc_ref.shape[0]
    n_keys = ka_scr.shape[0]
    chunks = [(r0, min(TK, n_ctx - r0)) for r0 in range(0, n_ctx, TK)]
    chunks += [(r0, TK) for r0 in range(n_ctx, n_keys, TK)]

    @pl.when(i == 0)
    def _fill():
        hrow = pl.ds(pl.multiple_of(h * HEAD_DIM, HEAD_DIM), HEAD_DIM)

        def put_k(kk, r0):
            lo = lax.broadcasted_iota(jnp.int32, kk.shape, 1) < HEAD_DIM
            kk = kk.astype(F32)
            ksw = pltpu.roll(kk, HEAD_DIM, 1)
            k_lo = jnp.where(h == 0, kk, ksw)
            k_hi = jnp.where(h == 0, ksw, kk)
            ka_scr[r0:r0 + kk.shape[0], :] = jnp.where(lo, k_lo, 0.0).astype(BF16)
            kb_scr[r0:r0 + kk.shape[0], :] = jnp.where(lo, 0.0, k_hi).astype(BF16)

        for r0, nk in chunks:
            if r0 < n_ctx:
                put_k(kc_ref[r0:r0 + nk, :], r0)
                vx_scr[0:HEAD_DIM, r0:r0 + nk] = vc_ref[hrow, r0:r0 + nk]
            else:
                put_k(k_ref[r0 - n_ctx:r0 - n_ctx + nk, :], r0)
                vx_scr[0:HEAD_DIM, r0:r0 + nk] = v_ref[hrow, r0 - n_ctx:r0 - n_ctx + nk]
            ones_row = lax.broadcasted_iota(jnp.int32, (VX_ROWS - HEAD_DIM, nk), 0) == 0
            vx_scr[HEAD_DIM:, r0:r0 + nk] = jnp.where(ones_row, 1.0, 0.0).astype(BF16)

    m_scr[...] = jnp.full(m_scr.shape, NEG_INIT, F32)
    acc_scr[...] = jnp.zeros(acc_scr.shape, F32)
    heads = range(m_scr.shape[0])

    def scores(c, g):
        r0, nk = chunks[c]
        k_scr = kb_scr if g % 2 else ka_scr
        s = _dot(k_scr[r0:r0 + nk, :], q_ref[g // 2])
        s_scr[c % S_SLOTS, g, 0:nk, :] = s
        mx_scr[c % S_SLOTS, g] = jnp.max(s, axis=0, keepdims=True)

    def softmax_pv(c, g):
        r0, nk = chunks[c]
        m_old = m_scr[g]
        m_new = jnp.maximum(m_old, mx_scr[c % S_SLOTS, g])
        alpha = jnp.exp2(m_old - m_new)
        for q0 in range(0, tq, Q_HALF):
            qs = slice(q0, q0 + Q_HALF)
            p = jnp.exp2(s_scr[c % S_SLOTS, g, 0:nk, qs] - m_new[:, qs]).astype(BF16)
            acc_scr[g, :, qs] = alpha[:, qs] * acc_scr[g, :, qs] + _dot(vx_scr[:, r0:r0 + nk], p)
        m_scr[g] = m_new

    ahead = S_SLOTS - 1
    for c in range(min(ahead, len(chunks))):
        for g in heads:
            scores(c, g)
    for c in range(len(chunks)):
        for g in heads:
            softmax_pv(c, g)
            if c + ahead < len(chunks):
                scores(c + ahead, g)

    for pair in range(len(heads) // 2):
        a = acc_scr[2 * pair]
        b = acc_scr[2 * pair + 1]
        st = jnp.concatenate([a[0:HEAD_DIM] / a[HEAD_DIM:HEAD_DIM + 1], b[0:HEAD_DIM] / b[HEAD_DIM:HEAD_DIM + 1]],
                             axis=0)
        o_ref[:, pair * LANES:(pair + 1) * LANES] = st.T.astype(BF16)


def _attention(q, k, kc, v, vc):
    b, npair, _, s = q.shape
    m = kc.shape[1]
    kvw = k.shape[2]
    pairs_per_kv = npair // N_KV_HEADS
    gw = pairs_per_kv * LANES
    assert s % TK == 0 and m % LANES == 0, (m, s, TK)
    full = lambda bi, h, i: (bi, 0, 0)
    return pl.pallas_call(
        _attn_kernel,
        grid=(b, N_KV_HEADS, s // TQ),
        in_specs=[
            pl.BlockSpec((None, pairs_per_kv, LANES, TQ), lambda bi, h, i: (bi, h, 0, i)),
            pl.BlockSpec((None, s, kvw), full),
            pl.BlockSpec((None, m, kvw), full),
            pl.BlockSpec((None, kvw, s), full),
            pl.BlockSpec((None, kvw, m), full),
        ],
        out_specs=pl.BlockSpec((None, TQ, gw), lambda bi, h, i: (bi, i, h)),
        out_shape=jax.ShapeDtypeStruct((b, s, N_KV_HEADS * gw), BF16),
        scratch_shapes=[
            pltpu.VMEM((m + s, LANES), BF16),
            pltpu.VMEM((m + s, LANES), BF16),
            pltpu.VMEM((VX_ROWS, m + s), BF16),
            pltpu.VMEM((S_SLOTS, 2 * pairs_per_kv, TK, TQ), F32),
            pltpu.VMEM((S_SLOTS, 2 * pairs_per_kv, 1, TQ), F32),
            pltpu.VMEM((2 * pairs_per_kv, 1, TQ), F32),
            pltpu.VMEM((2 * pairs_per_kv, VX_ROWS, TQ), F32),
        ],
        compiler_params=_params(3),
        name="attn",
    )(q, k, kc, v, vc)


def _ffn_residual(h, mod_ref, g_ref, wi_ref, wo_ref, xm_scr, acc_scr):
    dff = wo_ref.shape[0]
    xm_scr[...] = _rms_mod(h, g_ref[...], mod_ref[4:5, :], mod_ref[3:4, :]).astype(BF16)
    acc_scr[...] = jnp.zeros(acc_scr.shape, F32)

    for c0 in range(0, dff, FFN_CHUNK):
        xm = xm_scr[...]
        g = _dot(xm, wi_ref[:, c0:c0 + FFN_CHUNK])
        u = _dot(xm, wi_ref[:, dff + c0:dff + c0 + FFN_CHUNK])
        a = (g * _sigmoid(g) * u).astype(BF16)
        acc_scr[...] += _dot(a, wo_ref[c0:c0 + FFN_CHUNK, :])
    return h + mod_ref[5:6, :] * acc_scr[...]


def _layer_spec(w, layer):
    return pl.BlockSpec((None,) + w.shape[1:], lambda *_: (layer, 0, 0), pipeline_mode=pl.Buffered(1))


def _ffn_specs(d, wi, wd, layer):
    return [_const_spec((1, d)), _layer_spec(wi, layer), _layer_spec(wd, layer)]


def _ffn_scratch(d):
    return [pltpu.VMEM((TM, d), BF16), pltpu.VMEM((TM, d), F32)]


def _l0_post_kernel(x_ref, a_ref, sg_ref, mod_ref, wo_ref, gf_ref, wi_ref, wd_ref, mod1_ref, g1_ref, w1_ref, b1_ref,
                    h_ref, u_ref, xm_scr, acc_scr):
    aw = a_ref.shape[1]
    d = x_ref.shape[1]
    y = _dot(a_ref[...], wo_ref[0:aw, :]) + _dot(sg_ref[...], wo_ref[aw:, :])
    h = x_ref[...] + mod_ref[2:3, :] * y
    h = _ffn_residual(h, mod_ref, gf_ref, wi_ref, wd_ref, xm_scr, acc_scr)
    h_ref[...] = h
    xm_scr[...] = _rms_mod(h, g1_ref[...], mod1_ref[1:2, :], mod1_ref[0:1, :]).astype(BF16)
    for c0 in range(0, d, FFN_CHUNK):
        xm = xm_scr[...]
        a = _dot(xm, w1_ref[:, c0:c0 + FFN_CHUNK]) + b1_ref[:, c0:c0 + FFN_CHUNK]
        gate = _dot(xm, w1_ref[:, d + c0:d + c0 + FFN_CHUNK]) + b1_ref[:, d + c0:d + c0 + FFN_CHUNK]
        u_ref[:, c0:c0 + FFN_CHUNK] = a * _sigmoid(gate)


def _l0_post(x, attn, sg, mod, w_out, g_ffn, wi, wd, mod1, g_mix1, w_pw1, b_pw1):
    b, s, d = x.shape
    row = lambda bi, i: (bi, i, 0)
    mods = pl.BlockSpec((None, 6, d), lambda bi, i: (bi, 0, 0))
    tile = pl.BlockSpec((None, TM, d), row)
    return pl.pallas_call(
        _l0_post_kernel,
        grid=(b, s // TM),
        in_specs=[
            tile,
            pl.BlockSpec((None, TM, attn.shape[2]), row),
            pl.BlockSpec((None, TM, sg.shape[2]), row),
            mods,
            _const_spec(w_out.shape),
        ] + _ffn_specs(d, wi, wd, 0) + [mods, _const_spec((1, d)), _const_spec(w_pw1.shape), _const_spec((1, 2 * d))],
        out_specs=[tile, tile],
        out_shape=[jax.ShapeDtypeStruct((b, s, d), F32), jax.ShapeDtypeStruct((b, s, d), F32)],
        scratch_shapes=_ffn_scratch(d),
        compiler_params=_params(2),
        name="l0_post",
    )(x, attn, sg, mod, w_out, g_ffn, wi, wd, mod1, g_mix1, w_pw1, b_pw1)


def _l1_post_kernel(h_ref, u_ref, up_ref, un_ref, mod_ref, wdw_ref, bdw_ref, lng_ref, lnb_ref, w2_ref, b2_ref,
                    gf_ref, wi_ref, wd_ref, gfin_ref, o_ref, e_scr, c_scr, xm_scr, acc_scr):
    i = pl.program_id(1)
    n_i = pl.num_programs(1)
    tm = h_ref.shape[0]
    nslab = e_scr.shape[0]
    up = jnp.where(i > 0, up_ref[...], 0.0)
    un = jnp.where(i < n_i - 1, un_ref[...], 0.0)
    for sl in range(nslab):
        cols = slice(sl * LANES, (sl + 1) * LANES)
        e_scr[sl, 0:HALO, :] = up[:, cols]
        e_scr[sl, HALO:HALO + tm, :] = u_ref[:, cols]
        e_scr[sl, HALO + tm:, :] = un[:, cols]
    rb = 64
    off = HALO - CONV_WIDTH // 2

    def conv_slab(sl, carry):
        for r0 in range(0, tm, rb):
            acc = jnp.broadcast_to(bdw_ref[sl], (rb, LANES))
            for j in range(CONV_WIDTH):
                acc = acc + wdw_ref[sl, j:j + 1, :] * e_scr[sl, r0 + off + j:r0 + off + j + rb, :]
            c_scr[sl, r0:r0 + rb, :] = acc
        return carry

    lax.fori_loop(0, nslab, conv_slab, 0)
    cv = jnp.concatenate([c_scr[sl] for sl in range(nslab)], axis=1)
    mu = jnp.mean(cv, axis=-1, keepdims=True)
    dv = cv - mu
    var = jnp.mean(dv * dv, axis=-1, keepdims=True)
    ln = dv * lax.rsqrt(var + EPS) * lng_ref[...] + lnb_ref[...]
    act = (ln * _sigmoid(ln)).astype(BF16)
    y = _dot(act, w2_ref[...]) + b2_ref[...]
    h = h_ref[...] + mod_ref[2:3, :] * y
    h = _ffn_residual(h, mod_ref, gf_ref, wi_ref, wd_ref, xm_scr, acc_scr)
    ms = jnp.mean(h * h, axis=-1, keepdims=True)
    o_ref[...] = h * lax.rsqrt(ms + EPS) * gfin_ref[...]


def _l1_post(h, u, mod, w_dw, b_dw, ln_g, ln_b, w_pw2, b_pw2, g_ffn, wi, wd, g_final):
    b, s, d = h.shape
    row = lambda bi, i: (bi, i, 0)
    per = TM // HALO
    last = s // HALO - 1
    vec = _const_spec((1, d))
    return pl.pallas_call(
        _l1_post_kernel,
        grid=(b, s // TM),
        in_specs=[
            pl.BlockSpec((None, TM, d), row),
            pl.BlockSpec((None, TM, d), row),
            pl.BlockSpec((None, HALO, d), lambda bi, i: (bi, jnp.maximum(i * per - 1, 0), 0)),
            pl.BlockSpec((None, HALO, d), lambda bi, i: (bi, jnp.minimum((i + 1) * per, last), 0)),
            pl.BlockSpec((None, 6, d), lambda bi, i: (bi, 0, 0)),
            _const_spec(w_dw.shape), _const_spec(b_dw.shape), vec, vec,
            _const_spec(w_pw2.shape), vec,
        ] + _ffn_specs(d, wi, wd, 1) + [vec],
        out_specs=pl.BlockSpec((None, TM, d), row),
        out_shape=jax.ShapeDtypeStruct((b, s, d), F32),
        scratch_shapes=[pltpu.VMEM((d // LANES, TM + 2 * HALO, LANES), F32),
                        pltpu.VMEM((d // LANES, TM, LANES), F32)] + _ffn_scratch(d),
        compiler_params=_params(2),
        name="l1_post",
    )(h, u, u, u, mod, w_dw, b_dw, ln_g, ln_b, w_pw2, b_pw2, g_ffn, wi, wd, g_final)


def _rope_tables(n):
    half = HEAD_DIM // 2
    pos = np.arange(n)
    row = (pos // GRID_W).astype(np.float32)
    col = (pos % GRID_W).astype(np.float32)
    inv = (ROPE_THETA ** (-np.arange(0, half, 2, dtype=np.float32) / half)).astype(np.float32)
    lane = np.arange(LANES)
    hd = lane % HEAD_DIM
    w = hd % half
    ang = np.where((hd < half)[None, :], row[:, None], col[:, None]) * inv[w % (half // 2)][None, :]
    first = (w < half // 2)[None, :]
    sin = np.sin(ang)
    tabs = (np.cos(ang), np.where(first, -sin, 0.0), np.where(first, 0.0, sin))
    return tuple(jnp.asarray(t, F32) for t in tabs)


def kernel(x, c, ctx, c_ctx, w_mod, b_mod, g_mix, g_ffn, w_ffn_in, w_ffn_out, w_in, q_gain, k_gain, w_sp, b_sp,
           w_out, w_pw1, b_pw1, w_dw, b_dw, ln_g, ln_b, w_pw2, b_pw2, g_final):
    b, s, d = x.shape
    aw = N_Q_HEADS * HEAD_DIM
    kvw = N_KV_HEADS * HEAD_DIM

    rows = -(-(b + 1) // 8) * 8
    cvec = jnp.concatenate([c, c_ctx[None, :], jnp.zeros((rows - b - 1, d), F32)], axis=0)
    mod = _adaln(cvec, w_mod, b_mod)
    mod0 = mod[0, :b].reshape(b, 6, d)
    cmod0 = mod[0, b].reshape(6, d)
    mod1 = mod[1, :b].reshape(b, 6, d)

    cos, sup, sdn = _rope_tables(s)
    lane = jnp.arange(LANES)
    bd = (lane[:, None] // HEAD_DIM == lane[None, :] // HEAD_DIM).astype(BF16)
    qg2 = jnp.tile(q_gain[0], LANES // HEAD_DIM)[None, :]
    kg2 = jnp.tile(k_gain[0], LANES // HEAD_DIM)[None, :]
    w_in_b = w_in[0].astype(BF16)
    bsp = jnp.broadcast_to(b_sp[0][:, :, None], (N_SG_GROUPS, CHUNK, LANES))

    q, k, v, sg = _l0_pre(x, mod0, g_mix[0:1], w_in_b, qg2, kg2, cos, sup, sdn, bd, w_sp[0].astype(BF16), bsp)
    kc, vc = _ctx_kv(ctx, cmod0, g_mix[0:1], w_in_b[:, aw:aw + 2 * kvw], kg2, bd)
    attn = _attention(q, k, kc, v, vc)

    assert w_ffn_out.shape[1] % FFN_CHUNK == 0 and d % FFN_CHUNK == 0, (w_ffn_out.shape, d)
    wi_b = w_ffn_in.astype(BF16)
    wd_b = w_ffn_out.astype(BF16)
    h, u = _l0_post(x, attn, sg, mod0, w_out[0].astype(BF16), g_ffn[0:1], wi_b, wd_b,
                    mod1, g_mix[1:2], w_pw1[0].astype(BF16), b_pw1[0:1])
    w_dw_s = w_dw[0].reshape(CONV_WIDTH, d // LANES, LANES).transpose(1, 0, 2)
    b_dw_s = b_dw[0].reshape(d // LANES, 1, LANES)
    return _l1_post(h, u, mod1, w_dw_s, b_dw_s, ln_g[0:1], ln_b[0:1], w_pw2[0].astype(BF16), b_pw2[0:1],
                    g_ffn[1:2], wi_b, wd_b, g_final[None, :])
```
